```python
import math
import jax
import jax.numpy as jnp
from jax import lax
import numpy as np

D_MODEL = 2048
BATCH = 4
SEQ = 8192
DEPTH = 4
DEC_BATCH = 32
DEC_SEQ = 32
PAST_LEN = 2048

CHUNK = 64
Q_BLOCK = 128
N_MIXERS = 4
N_RET = (DEPTH + 3) // N_MIXERS
N_MLA = (DEPTH + 2) // N_MIXERS
N_S5 = (DEPTH + 1) // N_MIXERS
N_SB = DEPTH // N_MIXERS

RET_HEADS = 8
RET_DK = D_MODEL // RET_HEADS
RET_DV = 2 * D_MODEL // RET_HEADS
RET_QK = RET_HEADS * RET_DK
RET_V = RET_HEADS * RET_DV

MLA_HEADS = 16
MLA_NOPE = 128
MLA_ROPE = 64
MLA_VD = 128
MLA_Q_LORA = 512
MLA_KV_LORA = 512

S5_GROUP = 16
S5_GROUPS = D_MODEL // S5_GROUP
S5_STATE = 64

SB_HEADS = 16
SB_HD = D_MODEL // SB_HEADS

D_FF = 5632
CONV_W = 3
ROPE_BASE = 10000.0
NORM_EPS = 1e-6
GN_EPS = 1e-5
F32 = jnp.float32

kernel_name = 'hybrid_streaming_encoder_step'


def rms_norm(x, g):
    xf = x.astype(F32)
    y = xf * lax.rsqrt(jnp.mean(xf * xf, axis=-1, keepdims=True) + NORM_EPS)
    return (y * g.astype(F32)).astype(x.dtype)


def rope(x, pos):
    d = x.shape[-1]
    half = d // 2
    inv = ROPE_BASE ** (-jnp.arange(half, dtype=F32) * 2.0 / d)
    ang = pos.astype(F32)[:, None] * inv[None, :]
    shape = (pos.shape[0],) + (1,) * (x.ndim - 3) + (half,)
    cos, sin = jnp.cos(ang).reshape(shape), jnp.sin(ang).reshape(shape)
    xf = x.astype(F32)
    x1, x2 = xf[..., :half], xf[..., half:]
    return jnp.concatenate([x1 * cos - x2 * sin, x1 * sin + x2 * cos], axis=-1).astype(x.dtype)


def ret_log_gamma():
    return jnp.log1p(-jnp.exp2(-5.0 - jnp.arange(RET_HEADS, dtype=F32)))


def retention_block(q, k, v, s0):
    L = q.shape[1]
    dt = q.dtype
    log_g = ret_log_gamma()
    idx = jnp.arange(L, dtype=F32)
    diff = idx[:, None] - idx[None, :]
    decay = jnp.where(diff >= 0, jnp.exp(jnp.maximum(diff, 0.0)[None] * log_g[:, None, None]), 0.0)
    scores = jnp.einsum('blhd,bmhd->bhlm', q, k) * decay.astype(dt)
    o = jnp.einsum('bhlm,bmhe->blhe', scores, v)
    q_dec = jnp.exp((idx + 1.0)[:, None] * log_g[None, :]).astype(dt)
    o = o + jnp.einsum('blhd,bhde->blhe', q * q_dec[None, :, :, None], s0)
    k_dec = jnp.exp((L - 1.0 - idx)[:, None] * log_g[None, :]).astype(dt)
    s1 = (jnp.exp(L * log_g).astype(dt)[None, :, None, None] * s0
          + jnp.einsum('blhd,blhe->bhde', k * k_dec[None, :, :, None], v))
    return o, s1


def retention_mix(h, pos, s0, w_in, w_out):
    b, L, _ = h.shape
    q, k, v, g = jnp.split(h @ w_in, [RET_QK, 2 * RET_QK, 2 * RET_QK + RET_V], axis=-1)
    q = rope(q.reshape(b, L, RET_HEADS, RET_DK), pos)
    k = rope(k.reshape(b, L, RET_HEADS, RET_DK), pos) * (RET_DK ** -0.5)
    v = v.reshape(b, L, RET_HEADS, RET_DV)
    if L <= CHUNK:
        o, s1 = retention_block(q, k, v, s0)
    else:
        n = L // CHUNK
        to_blocks = lambda t: jnp.moveaxis(t.reshape(b, n, CHUNK, *t.shape[2:]), 1, 0)

        def step(s, blk):
            o_c, s_n = retention_block(blk[0], blk[1], blk[2], s)
            return s_n, o_c

        s1, o = lax.scan(step, s0, (to_blocks(q), to_blocks(k), to_blocks(v)))
        o = jnp.moveaxis(o, 0, 1).reshape(b, L, RET_HEADS, RET_DV)
    of = o.astype(F32)
    mu = jnp.mean(of, axis=-1, keepdims=True)
    var = jnp.mean(jnp.square(of - mu), axis=-1, keepdims=True)
    of = (of - mu) * lax.rsqrt(var + GN_EPS)
    y = jax.nn.silu(g) * of.reshape(b, L, RET_V).astype(h.dtype)
    return y @ w_out, s1


def mla_project(h, pos, w_down, g_q, g_kv, w_uq):
    b, L, _ = h.shape
    cq, ckv, kr = jnp.split(h @ w_down, [MLA_Q_LORA, MLA_Q_LORA + MLA_KV_LORA], axis=-1)
    q = (rms_norm(cq, g_q) @ w_uq).reshape(b, L, MLA_HEADS, MLA_NOPE + MLA_ROPE)
    q_nope, q_rope = q[..., :MLA_NOPE], rope(q[..., MLA_NOPE:], pos)
    return q_nope, q_rope, rms_norm(ckv, g_kv), rope(kr, pos)


def mla_prompt(h, w_down, g_q, g_kv, w_uq, w_uk, w_uv, w_o):
    b, L, _ = h.shape
    pos = jnp.arange(L)
    q_nope, q_rope, ckv, kr = mla_project(h, pos, w_down, g_q, g_kv, w_uq)
    k_nope = jnp.einsum('bsc,chd->bshd', ckv, w_uk)
    v = jnp.einsum('bsc,chd->bshd', ckv, w_uv)
    key_chunk = pos // CHUNK
    scale = (MLA_NOPE + MLA_ROPE) ** -0.5

    def block(start):
        qn = lax.dynamic_slice_in_dim(q_nope, start, Q_BLOCK, axis=1)
        qr = lax.dynamic_slice_in_dim(q_rope, start, Q_BLOCK, axis=1)
        s = (jnp.einsum('bqhd,bkhd->bhqk', qn, k_nope)
             + jnp.einsum('bqhr,bkr->bhqk', qr, kr)).astype(F32) * scale
        q_chunk = (start + jnp.arange(Q_BLOCK)) // CHUNK
        s = jnp.where(key_chunk[None, :] <= q_chunk[:, None], s, -jnp.inf)
        p = jax.nn.softmax(s, axis=-1).astype(v.dtype)
        return jnp.einsum('bhqk,bkhd->bqhd', p, v)

    o = lax.map(block, jnp.arange(0, L, Q_BLOCK))
    o = jnp.moveaxis(o, 0, 1).reshape(b, L, MLA_HEADS * MLA_VD)
    return o @ w_o, ckv, kr


def mla_sample(h, cache_ckv, cache_kr, w_down, g_q, g_kv, w_uq, w_uk, w_uv, w_o):
    b, T, _ = h.shape
    pos = PAST_LEN + jnp.arange(T)
    q_nope, q_rope, ckv, kr = mla_project(h, pos, w_down, g_q, g_kv, w_uq)
    ckv_all = jnp.concatenate([cache_ckv, ckv], axis=1)
    kr_all = jnp.concatenate([cache_kr, kr], axis=1)
    q_lat = jnp.einsum('bthd,chd->bthc', q_nope, w_uk)
    s = (jnp.einsum('bthc,bkc->bhtk', q_lat, ckv_all)
         + jnp.einsum('bthr,bkr->bhtk', q_rope, kr_all)).astype(F32) * ((MLA_NOPE + MLA_ROPE) ** -0.5)
    p = jax.nn.softmax(s, axis=-1).astype(h.dtype)
    o_lat = jnp.einsum('bhtk,bkc->bthc', p, ckv_all)
    o = jnp.einsum('bthc,chd->bthd', o_lat, w_uv).reshape(b, T, MLA_HEADS * MLA_VD)
    return o @ w_o, ckv, kr


def s5_discretise(a_re, a_im, log_dt, b_re, b_im):
    lam = lax.complex(a_re.astype(F32), a_im.astype(F32))
    step = jnp.exp(log_dt.astype(F32))[:, None]
    a_bar = jnp.exp(lam * step)
    b_bar = ((a_bar - 1.0) / lam)[..., None] * lax.complex(b_re.astype(F32), b_im.astype(F32))
    return a_bar, b_bar


def s5_block(u, h0, a_bar, b_bar, c):
    bu = jnp.einsum('blgc,gpc->blgp', u.astype(jnp.complex64), b_bar)
    a = jnp.broadcast_to(a_bar, bu.shape)

    def combine(left, right):
        return left[0] * right[0], right[0] * left[1] + right[1]

    a_cum, hs = lax.associative_scan(combine, (a, bu), axis=1)
    hs = hs + a_cum * h0[:, None]
    y = jnp.einsum('blgp,gcp->blgc', hs, c).real
    return y, hs[:, -1]


def s5_mix(h, h0_re, h0_im, a_re, a_im, log_dt, b_re, b_im, c_re, c_im, d_skip, w_glu, b_glu):
    b, L, _ = h.shape
    a_bar, b_bar = s5_discretise(a_re, a_im, log_dt, b_re, b_im)
    c = lax.complex(c_re.astype(F32), c_im.astype(F32))
    h0 = lax.complex(h0_re.astype(F32), h0_im.astype(F32))
    u = h.astype(F32).reshape(b, L, S5_GROUPS, S5_GROUP)
    if L <= CHUNK:
        y, h1 = s5_block(u, h0, a_bar, b_bar, c)
    else:
        n = L // CHUNK
        uc = jnp.moveaxis(u.reshape(b, n, CHUNK, S5_GROUPS, S5_GROUP), 1, 0)

        def step(hc, u_blk):
            y_blk, hn = s5_block(u_blk, hc, a_bar, b_bar, c)
            return hn, y_blk

        h1, y = lax.scan(step, h0, uc)
        y = jnp.moveaxis(y, 0, 1)
    y = y.reshape(b, L, D_MODEL) + d_skip.astype(F32) * h.astype(F32)
    z = jax.nn.gelu(y).astype(h.dtype) @ w_glu + b_glu
    val, gate = jnp.split(z, 2, axis=-1)
    return val * jax.nn.sigmoid(gate), h1.real, h1.imag


def sb_weights(q, k, q_pos, k_pos):
    z = jnp.einsum('bqhd,bkhd->bhqk', q, k).astype(F32) * (SB_HD ** -0.5)
    mask = k_pos[None, :] < q_pos[:, None]
    sp = jnp.where(mask, jax.nn.softplus(z), 0.0)
    log_w = z - lax.cumsum(sp, axis=3, reverse=True)
    return jnp.exp(jnp.where(mask, log_w, -jnp.inf))


def sb_qkv(h, w_qkv):
    b, L, _ = h.shape
    q, k, v = (t.reshape(b, L, SB_HEADS, SB_HD) for t in jnp.split(h @ w_qkv, 3, axis=-1))
    return q, k, v


def sb_prompt(h, w_qkv, w_o):
    b, L, _ = h.shape
    q, k, v = sb_qkv(h, w_qkv)
    pos = jnp.arange(L)

    def block(start):
        qb = lax.dynamic_slice_in_dim(q, start, Q_BLOCK, axis=1)
        w = sb_weights(qb, k, start + jnp.arange(Q_BLOCK), pos).astype(v.dtype)
        return jnp.einsum('bhqk,bkhd->bqhd', w, v)

    o = lax.map(block, jnp.arange(0, L, Q_BLOCK))
    o = jnp.moveaxis(o, 0, 1).reshape(b, L, SB_HEADS * SB_HD)
    return o @ w_o, k, v


def sb_sample(h, cache_k, cache_v, w_qkv, w_o):
    b, T, _ = h.shape
    q, k, v = sb_qkv(h, w_qkv)
    k_all = jnp.concatenate([cache_k, k], axis=1)
    v_all = jnp.concatenate([cache_v, v], axis=1)
    w = sb_weights(q, k_all, PAST_LEN + jnp.arange(T), jnp.arange(PAST_LEN + T)).astype(v.dtype)
    o = jnp.einsum('bhqk,bkhd->bqhd', w, v_all).reshape(b, T, SB_HEADS * SB_HD)
    return o @ w_o, k, v


def conv_ffn(h, conv_state, w_up, conv_w, conv_b, w_down):
    L = h.shape[1]
    ext = jnp.concatenate([conv_state, h @ w_up], axis=1)
    c = conv_b + sum(conv_w[t] * ext[:, t:t + L] for t in range(CONV_W))
    val, gate = jnp.split(c, 2, axis=-1)
    return (jax.nn.gelu(gate, approximate=True) * val) @ w_down, ext[:, L:]


def setup_inputs(seed: int = 0) -> dict:
    key = jax.random.key(seed)
    ks = iter(jax.random.split(key, 40))

    def nrm(shape, scale):
        return jax.random.normal(next(ks), shape, F32) * scale

    def gain(shape):
        return 1.0 + nrm(shape, 0.05)

    two_f = 2 * D_FF
    s5_shape = (N_S5, S5_GROUPS, S5_STATE)
    return {
        'x_prompt': nrm((BATCH, SEQ, D_MODEL), 1.0),
        'x_sample': nrm((DEC_BATCH, DEC_SEQ, D_MODEL), 1.0),
        'state_ret': nrm((N_RET, DEC_BATCH, RET_HEADS, RET_DK, RET_DV), 0.5),
        'cache_mla_ckv': nrm((N_MLA, DEC_BATCH, PAST_LEN, MLA_KV_LORA), 1.0),
        'cache_mla_krope': nrm((N_MLA, DEC_BATCH, PAST_LEN, MLA_ROPE), 1.0),
        'state_s5_re': nrm((N_S5, DEC_BATCH, S5_GROUPS, S5_STATE), 0.1),
        'state_s5_im': nrm((N_S5, DEC_BATCH, S5_GROUPS, S5_STATE), 0.1),
        'cache_sb_k': nrm((N_SB, DEC_BATCH, PAST_LEN, SB_HEADS, SB_HD), 1.0),
        'cache_sb_v': nrm((N_SB, DEC_BATCH, PAST_LEN, SB_HEADS, SB_HD), 1.0),
        'state_ffn_conv': nrm((DEPTH, DEC_BATCH, CONV_W - 1, two_f), 1.0),
        'norm_gains': gain((DEPTH, 4, D_MODEL)),
        'ret_w_in': nrm((N_RET, D_MODEL, 2 * RET_QK + 2 * RET_V), D_MODEL ** -0.5),
        'ret_w_out': nrm((N_RET, RET_V, D_MODEL), RET_V ** -0.5),
        'mla_w_down': nrm((N_MLA, D_MODEL, MLA_Q_LORA + MLA_KV_LORA + MLA_ROPE), D_MODEL ** -0.5),
        'mla_g_q': gain((N_MLA, MLA_Q_LORA)),
        'mla_g_kv': gain((N_MLA, MLA_KV_LORA)),
        'mla_w_uq': nrm((N_MLA, MLA_Q_LORA, MLA_HEADS * (MLA_NOPE + MLA_ROPE)), MLA_Q_LORA ** -0.5),
        'mla_w_uk': nrm((N_MLA, MLA_KV_LORA, MLA_HEADS, MLA_NOPE), MLA_KV_LORA ** -0.5),
        'mla_w_uv': nrm((N_MLA, MLA_KV_LORA, MLA_HEADS, MLA_VD), MLA_KV_LORA ** -0.5),
        'mla_w_o': nrm((N_MLA, MLA_HEADS * MLA_VD, D_MODEL), (MLA_HEADS * MLA_VD) ** -0.5),
        's5_a_re': -0.5 + nrm(s5_shape, 0.01),
        's5_a_im': np.pi * jnp.arange(S5_STATE, dtype=F32) + nrm(s5_shape, 0.01),
        's5_log_dt': jax.random.uniform(next(ks), (N_S5, S5_GROUPS), F32, math.log(1e-3), math.log(1e-1)),
        's5_b_re': nrm((N_S5, S5_GROUPS, S5_STATE, S5_GROUP), (2 * S5_GROUP) ** -0.5),
        's5_b_im': nrm((N_S5, S5_GROUPS, S5_STATE, S5_GROUP), (2 * S5_GROUP) ** -0.5),
        's5_c_re': nrm((N_S5, S5_GROUPS, S5_GROUP, S5_STATE), 0.5),
        's5_c_im': nrm((N_S5, S5_GROUPS, S5_GROUP, S5_STATE), 0.5),
        's5_d': nrm((N_S5, D_MODEL), 1.0),
        's5_w_glu': nrm((N_S5, D_MODEL, 2 * D_MODEL), D_MODEL ** -0.5),
        's5_b_glu': nrm((N_S5, 2 * D_MODEL), 0.01),
        'sb_w_qkv': nrm((N_SB, D_MODEL, 3 * SB_HEADS * SB_HD), D_MODEL ** -0.5),
        'sb_w_o': nrm((N_SB, SB_HEADS * SB_HD, D_MODEL), (SB_HEADS * SB_HD) ** -0.5),
        'ffn_w_up': nrm((DEPTH, D_MODEL, two_f), D_MODEL ** -0.5),
        'ffn_conv_w': nrm((DEPTH, CONV_W, two_f), CONV_W ** -0.5),
        'ffn_conv_b': nrm((DEPTH, two_f), 0.01),
        'ffn_w_down': nrm((DEPTH, D_FF, D_MODEL), D_FF ** -0.5),
    }


def reference(x_prompt, x_sample, state_ret, cache_mla_ckv, cache_mla_krope, state_s5_re, state_s5_im,
              cache_sb_k, cache_sb_v, state_ffn_conv, norm_gains, ret_w_in, ret_w_out, mla_w_down, mla_g_q,
              mla_g_kv, mla_w_uq, mla_w_uk, mla_w_uv, mla_w_o, s5_a_re, s5_a_im, s5_log_dt, s5_b_re, s5_b_im,
              s5_c_re, s5_c_im, s5_d, s5_w_glu, s5_b_glu, sb_w_qkv, sb_w_o, ffn_w_up, ffn_conv_w, ffn_conv_b,
              ffn_w_down):
    xp, xs = x_prompt, x_sample
    n_p, seq_p, _ = xp.shape
    pos_p = jnp.arange(seq_p)
    pos_s = PAST_LEN + jnp.arange(xs.shape[1])
    ret_p, ret_s, ckv_p, ckv_s, kr_p, kr_s = [], [], [], [], [], []
    s5re_p, s5re_s, s5im_p, s5im_s = [], [], [], []
    sbk_p, sbk_s, sbv_p, sbv_s, conv_p, conv_s = [], [], [], [], [], []
    for i in range(DEPTH):
        j = i // N_MIXERS
        kind = i % N_MIXERS
        g = norm_gains[i]
        hp, hs = rms_norm(xp, g[0]), rms_norm(xs, g[0])
        if kind == 0:
            zero = jnp.zeros((n_p, RET_HEADS, RET_DK, RET_DV), xp.dtype)
            mp, st = retention_mix(hp, pos_p, zero, ret_w_in[j], ret_w_out[j])
            ret_p.append(st)
            ms, st = retention_mix(hs, pos_s, state_ret[j], ret_w_in[j], ret_w_out[j])
            ret_s.append(st)
        elif kind == 1:
            mla_w = (mla_w_down[j], mla_g_q[j], mla_g_kv[j], mla_w_uq[j], mla_w_uk[j], mla_w_uv[j], mla_w_o[j])
            mp, c_new, r_new = mla_prompt(hp, *mla_w)
            ckv_p.append(c_new)
            kr_p.append(r_new)
            ms, c_new, r_new = mla_sample(hs, cache_mla_ckv[j], cache_mla_krope[j], *mla_w)
            ckv_s.append(c_new)
            kr_s.append(r_new)
        elif kind == 2:
            s5_w = (s5_a_re[j], s5_a_im[j], s5_log_dt[j], s5_b_re[j], s5_b_im[j], s5_c_re[j], s5_c_im[j],
                    s5_d[j], s5_w_glu[j], s5_b_glu[j])
            zero = jnp.zeros((n_p, S5_GROUPS, S5_STATE), F32)
            mp, re_new, im_new = s5_mix(hp, zero, zero, *s5_w)
            s5re_p.append(re_new)
            s5im_p.append(im_new)
            ms, re_new, im_new = s5_mix(hs, state_s5_re[j], state_s5_im[j], *s5_w)
            s5re_s.append(re_new)
            s5im_s.append(im_new)
        else:
            mp, k_new, v_new = sb_prompt(hp, sb_w_qkv[j], sb_w_o[j])
            sbk_p.append(k_new)
            sbv_p.append(v_new)
            ms, k_new, v_new = sb_sample(hs, cache_sb_k[j], cache_sb_v[j], sb_w_qkv[j], sb_w_o[j])
            sbk_s.append(k_new)
            sbv_s.append(v_new)
        xp = xp + rms_norm(mp, g[1])
        xs = xs + rms_norm(ms, g[1])
        hp, hs = rms_norm(xp, g[2]), rms_norm(xs, g[2])
        zero = jnp.zeros((n_p, CONV_W - 1, 2 * D_FF), xp.dtype)
        fp, cp = conv_ffn(hp, zero, ffn_w_up[i], ffn_conv_w[i], ffn_conv_b[i], ffn_w_down[i])
        fs, cs = conv_ffn(hs, state_ffn_conv[i], ffn_w_up[i], ffn_conv_w[i], ffn_conv_b[i], ffn_w_down[i])
        conv_p.append(cp)
        conv_s.append(cs)
        xp = xp + rms_norm(fp, g[3])
        xs = xs + rms_norm(fs, g[3])
    return (xp, xs, jnp.stack(ret_p), jnp.stack(ret_s), jnp.stack(ckv_p), jnp.stack(ckv_s),
            jnp.stack(kr_p), jnp.stack(kr_s), jnp.stack(s5re_p), jnp.stack(s5re_s),
            jnp.stack(s5im_p), jnp.stack(s5im_s), jnp.stack(sbk_p), jnp.stack(sbk_s),
            jnp.stack(sbv_p), jnp.stack(sbv_s), jnp.stack(conv_p), jnp.stack(conv_s))
```

```python
import functools
import math

import jax
import jax.numpy as jnp
from jax import lax
from jax.experimental import pallas as pl
from jax.experimental.pallas import tpu as pltpu

F32 = jnp.float32
BF16 = jnp.bfloat16
NORM_EPS = 1e-6
GN_EPS = 1e-5
ROPE_BASE = 10000.0
CHUNK = 64
CONV_W = 3
LANES = 128
VMEM_LIMIT = 56 * 1024 * 1024


def _cparams(*sem):
    return pltpu.CompilerParams(dimension_semantics=sem, vmem_limit_bytes=VMEM_LIMIT)


def _tile(n, pref):
    t = min(n, pref)
    while n % t:
        t -= 1
    return t


def _dot(a, b):
    return jnp.dot(a, b, preferred_element_type=F32)


def _dot_nt(a, b):
    return lax.dot_general(a, b, (((1,), (1,)), ((), ())), preferred_element_type=F32)


def _rms(x, g):
    return x * lax.rsqrt(jnp.mean(x * x, axis=-1, keepdims=True) + NORM_EPS) * g


def _gelu_tanh(x):
    return 0.5 * x * (1.0 + jnp.tanh(math.sqrt(2.0 / math.pi) * (x + 0.044715 * (x * x * x))))


def _softplus(z):
    return jnp.maximum(z, 0.0) + jnp.log(1.0 + jnp.exp(-jnp.abs(z)))


def _rmsnorm_kernel(x_ref, g_ref, o_ref):
    o_ref[...] = _rms(x_ref[...], g_ref[...]).astype(o_ref.dtype)


def _rmsnorm(x, g, out_dtype=BF16, tm=512):
    m, d = x.shape
    tm = _tile(m, tm)
    return pl.pallas_call(
        _rmsnorm_kernel,
        grid=(m // tm,),
        in_specs=[pl.BlockSpec((tm, d), lambda i: (i, 0)), pl.BlockSpec((1, d), lambda i: (0, 0))],
        out_specs=pl.BlockSpec((tm, d), lambda i: (i, 0)),
        out_shape=jax.ShapeDtypeStruct((m, d), out_dtype),
        compiler_params=_cparams("parallel"),
        name="rmsnorm",
    )(x, g.reshape(1, d))


def _linear_kernel(a_ref, w_ref, *o_refs, head_major):
    acc = _dot(a_ref[...].astype(BF16), w_ref[...])
    for o_ref in o_refs:
        if head_major and len(o_ref.shape) == 3:
            for hh in range(o_ref.shape[0]):
                o_ref[hh] = acc[:, hh * LANES:(hh + 1) * LANES].astype(o_ref.dtype)
        else:
            o_ref[...] = acc.astype(o_ref.dtype)


def _linear(a, w, col_off, n_out, out_dtypes, head_major=(), tm=1024, tn=512, name="linear"):
    m, k = a.shape
    tm, tn = _tile(m, tm), _tile(n_out, tn)
    assert col_off % tn == 0 and tn % LANES == 0
    joff = col_off // tn
    out_specs, out_shape = [], []
    for dt, hm in zip(out_dtypes, head_major or (False,) * len(out_dtypes)):
        if hm:
            out_specs.append(pl.BlockSpec((tn // LANES, tm, LANES), lambda i, j: (j, i, 0)))
            out_shape.append(jax.ShapeDtypeStruct((n_out // LANES, m, LANES), dt))
        else:
            out_specs.append(pl.BlockSpec((tm, tn), lambda i, j: (i, j)))
            out_shape.append(jax.ShapeDtypeStruct((m, n_out), dt))
    return pl.pallas_call(
        functools.partial(_linear_kernel, head_major=bool(head_major)),
        grid=(m // tm, n_out // tn),
        in_specs=[pl.BlockSpec((tm, k), lambda i, j: (i, 0)),
                  pl.BlockSpec((k, tn), lambda i, j: (0, joff + j))],
        out_specs=out_specs,
        out_shape=out_shape,
        compiler_params=_cparams("parallel", "arbitrary"),
        name=name,
    )(a, w)


def _linear_rope_kernel(a_ref, w_ref, cos_ref, sin_ref, o_ref, *, dk, n_q_tiles, k_scale):
    acc = _dot(a_ref[...], w_ref[...])
    scale = jnp.where(pl.program_id(1) >= n_q_tiles, k_scale, 1.0).astype(F32)
    cos, sin = cos_ref[...] * scale, sin_ref[...] * scale
    half = dk // 2
    for hh in range(acc.shape[1] // dk):
        x1 = acc[:, hh * dk:hh * dk + half]
        x2 = acc[:, hh * dk + half:(hh + 1) * dk]
        o_ref[:, hh * dk:hh * dk + half] = x1 * cos - x2 * sin
        o_ref[:, hh * dk + half:(hh + 1) * dk] = x1 * sin + x2 * cos


def _linear_rope(a, w, n_out, cos, sin, dk, n_q, k_scale, tm=1024, tn=512):
    m, k = a.shape
    ltab = cos.shape[0]
    tm = _tile(math.gcd(m, ltab), tm)
    tn = _tile(n_out, tn)
    ntab = ltab // tm
    return pl.pallas_call(
        functools.partial(_linear_rope_kernel, dk=dk, n_q_tiles=n_q // tn, k_scale=k_scale),
        grid=(m // tm, n_out // tn),
        in_specs=[pl.BlockSpec((tm, k), lambda i, j: (i, 0)),
                  pl.BlockSpec((k, tn), lambda i, j: (0, j)),
                  pl.BlockSpec((tm, dk // 2), lambda i, j: (i % ntab, 0)),
                  pl.BlockSpec((tm, dk // 2), lambda i, j: (i % ntab, 0))],
        out_specs=pl.BlockSpec((tm, tn), lambda i, j: (i, j)),
        out_shape=jax.ShapeDtypeStruct((m, n_out), F32),
        compiler_params=_cparams("parallel", "arbitrary"),
        name="ret_qk_rope",
    )(a, w, cos, sin)


def _outproj_kernel(a_ref, w_ref, x_ref, gp_ref, gn_ref, xo_ref, *rest, emit_h):
    acc_ref = rest[-1]
    kk = pl.program_id(1)

    @pl.when(kk == 0)
    def _():
        acc_ref[...] = jnp.zeros_like(acc_ref)

    acc_ref[...] += _dot(a_ref[...].astype(BF16), w_ref[...])

    @pl.when(kk == pl.num_programs(1) - 1)
    def _():
        xn = x_ref[...] + _rms(acc_ref[...], gp_ref[...])
        xo_ref[...] = xn
        if emit_h:
            rest[0][...] = _rms(xn, gn_ref[...]).astype(BF16)


def _outproj(a, w, x, g_post, g_next, emit_h=True, tm=512, tk=512, name="outproj"):
    m, k = a.shape
    d = w.shape[1]
    tm, tk = _tile(m, tm), _tile(k, tk)
    out_specs = [pl.BlockSpec((tm, d), lambda i, kk: (i, 0))]
    out_shape = [jax.ShapeDtypeStruct((m, d), F32)]
    if emit_h:
        out_specs.append(pl.BlockSpec((tm, d), lambda i, kk: (i, 0)))
        out_shape.append(jax.ShapeDtypeStruct((m, d), BF16))
    res = pl.pallas_call(
        functools.partial(_outproj_kernel, emit_h=emit_h),
        grid=(m // tm, k // tk),
        in_specs=[pl.BlockSpec((tm, tk), lambda i, kk: (i, kk)),
                  pl.BlockSpec((tk, d), lambda i, kk: (kk, 0)),
                  pl.BlockSpec((tm, d), lambda i, kk: (i, 0)),
                  pl.BlockSpec((1, d), lambda i, kk: (0, 0)),
                  pl.BlockSpec((1, d), lambda i, kk: (0, 0))],
        out_specs=out_specs,
        out_shape=out_shape,
        scratch_shapes=[pltpu.VMEM((tm, d), F32)],
        compiler_params=_cparams("parallel", "arbitrary"),
        name=name,
    )(a, w, x, g_post.reshape(1, d), g_next.reshape(1, d))
    return (res[0], res[1]) if emit_h else (res[0], None)


def _glu_outproj_kernel(a_ref, wv_ref, wg_ref, bv_ref, bg_ref, x_ref, gp_ref, gn_ref, xo_ref, ho_ref,
                        accv_ref, accg_ref):
    kk = pl.program_id(1)

    @pl.when(kk == 0)
    def _():
        accv_ref[...] = jnp.zeros_like(accv_ref)
        accg_ref[...] = jnp.zeros_like(accg_ref)

    a = a_ref[...]
    accv_ref[...] += _dot(a, wv_ref[...])
    accg_ref[...] += _dot(a, wg_ref[...])

    @pl.when(kk == pl.num_programs(1) - 1)
    def _():
        mix = (accv_ref[...] + bv_ref[...]) * jax.nn.sigmoid(accg_ref[...] + bg_ref[...])
        xn = x_ref[...] + _rms(mix, gp_ref[...])
        xo_ref[...] = xn
        ho_ref[...] = _rms(xn, gn_ref[...]).astype(BF16)


def _glu_outproj(a, w, b, x, g_post, g_next, tm=512, tk=512):
    m, k = a.shape
    d = w.shape[1] // 2
    tm, tk = _tile(m, tm), _tile(k, tk)
    row = lambda i, kk: (i, 0)
    const = lambda i, kk: (0, 0)
    return pl.pallas_call(
        _glu_outproj_kernel,
        grid=(m // tm, k // tk),
        in_specs=[pl.BlockSpec((tm, tk), lambda i, kk: (i, kk)),
                  pl.BlockSpec((tk, d), lambda i, kk: (kk, 0)),
                  pl.BlockSpec((tk, d), lambda i, kk: (kk, 1)),
                  pl.BlockSpec((1, d), lambda i, kk: (0, 0)),
                  pl.BlockSpec((1, d), lambda i, kk: (0, 1)),
                  pl.BlockSpec((tm, d), row), pl.BlockSpec((1, d), const), pl.BlockSpec((1, d), const)],
        out_specs=[pl.BlockSpec((tm, d), row), pl.BlockSpec((tm, d), row)],
        out_shape=[jax.ShapeDtypeStruct((m, d), F32), jax.ShapeDtypeStruct((m, d), BF16)],
        scratch_shapes=[pltpu.VMEM((tm, d), F32), pltpu.VMEM((tm, d), F32)],
        compiler_params=_cparams("parallel", "arbitrary"),
        name="s5_glu_outproj",
    )(a, w, w, b.reshape(1, 2 * d), b.reshape(1, 2 * d), x, g_post.reshape(1, d), g_next.reshape(1, d))


def _ffn_kernel(*refs, n_seq, has_state, emit_h, tiles_per_seq):
    (h_ref, x_ref, wv_ref, wg_ref, cwv_ref, cwg_ref, cbv_ref, cbg_ref, wd_ref, gp_ref, gn_ref), refs = refs[:11], refs[11:]
    if has_state:
        (stv_ref, stg_ref), refs = refs[:2], refs[2:]
    xo_ref, refs = refs[0], refs[1:]
    if emit_h:
        ho_ref, refs = refs[0], refs[1:]
    cv_ref, cg_ref, acc_ref = refs[:3]
    carry_refs = refs[3:]
    i, j = pl.program_id(0), pl.program_id(1)
    tm, tf = h_ref.shape[0], wv_ref.shape[1]
    seq = tm // n_seq
    h = h_ref[...]

    if not has_state:
        @pl.when((i == 0) & (j == 0))
        def _():
            for carry_ref in carry_refs:
                carry_ref[...] = jnp.zeros_like(carry_ref)

    def conv_half(w_ref, cw_ref, cb_ref, st_ref, carry_ref, c_out_ref):
        up = _dot(h, w_ref[...]).reshape(n_seq, seq, tf)
        if has_state:
            prev1, prev2 = st_ref[:, 1:2, :], st_ref[:, 0:1, :]
        else:
            old = jnp.where(i % tiles_per_seq != 0, carry_ref[j], 0.0)
            prev1, prev2 = old[7:8][None], old[6:7][None]
            carry_ref[j] = up[0, seq - 8:, :]
        pos = lax.broadcasted_iota(jnp.int32, up.shape, 1)
        p1 = jnp.where(pos == 0, prev1, pltpu.roll(up, 1, axis=1))
        p2 = jnp.where(pos == 0, prev2, jnp.where(pos == 1, prev1, pltpu.roll(up, 2, axis=1)))
        c_out_ref[...] = up[:, seq - 2:, :]
        cw = cw_ref[...]
        c = cb_ref[...][None] + cw[0:1][None] * p2 + cw[1:2][None] * p1 + cw[2:3][None] * up
        return c.reshape(tm, tf)

    val = conv_half(wv_ref, cwv_ref, cbv_ref, stv_ref if has_state else None,
                    None if has_state else carry_refs[0], cv_ref)
    gate = conv_half(wg_ref, cwg_ref, cbg_ref, stg_ref if has_state else None,
                     None if has_state else carry_refs[1], cg_ref)
    act = (_gelu_tanh(gate) * val).astype(BF16)

    @pl.when(j == 0)
    def _():
        acc_ref[...] = jnp.zeros_like(acc_ref)

    acc_ref[...] += _dot(act, wd_ref[...])

    @pl.when(j == pl.num_programs(1) - 1)
    def _():
        xn = x_ref[...] + _rms(acc_ref[...], gp_ref[...])
        xo_ref[...] = xn
        if emit_h:
            ho_ref[...] = _rms(xn, gn_ref[...]).astype(BF16)


def _conv_ffn(h, x, w_up, conv_w, conv_b, w_down, g_post, g_next, seq_len, state=None, emit_h=True, tm=512, tf=512):
    m, d = h.shape
    d_ff = w_down.shape[0]
    tf = _tile(d_ff, tf)
    nf = d_ff // tf
    has_state = state is not None
    if has_state:
        tm = _tile(m // seq_len, max(tm // seq_len, 1)) * seq_len
        n_seq, tiles_per_seq = tm // seq_len, 1
    else:
        tm = _tile(seq_len, tm)
        n_seq, tiles_per_seq = 1, seq_len // tm
    n_seqs = m // seq_len
    row = lambda i, j: (i, 0)
    const = lambda i, j: (0, 0)
    in_specs = [pl.BlockSpec((tm, d), row), pl.BlockSpec((tm, d), row),
                pl.BlockSpec((d, tf), lambda i, j: (0, j)), pl.BlockSpec((d, tf), lambda i, j: (0, nf + j)),
                pl.BlockSpec((CONV_W, tf), lambda i, j: (0, j)), pl.BlockSpec((CONV_W, tf), lambda i, j: (0, nf + j)),
                pl.BlockSpec((1, tf), lambda i, j: (0, j)), pl.BlockSpec((1, tf), lambda i, j: (0, nf + j)),
                pl.BlockSpec((tf, d), lambda i, j: (j, 0)),
                pl.BlockSpec((1, d), const), pl.BlockSpec((1, d), const)]
    args = [h, x, w_up, w_up, conv_w, conv_w, conv_b.reshape(1, -1), conv_b.reshape(1, -1), w_down,
            g_post.reshape(1, d), g_next.reshape(1, d)]
    if has_state:
        in_specs += [pl.BlockSpec((n_seq, CONV_W - 1, tf), lambda i, j: (i, 0, j)),
                     pl.BlockSpec((n_seq, CONV_W - 1, tf), lambda i, j: (i, 0, nf + j))]
        args += [state, state]
        scratch = []
    else:
        scratch = [pltpu.VMEM((nf, 8, tf), F32), pltpu.VMEM((nf, 8, tf), F32)]
    conv_map = lambda i, j: (i, 0, j)
    out_specs = [pl.BlockSpec((tm, d), row)]
    out_shape = [jax.ShapeDtypeStruct((m, d), F32)]
    if emit_h:
        out_specs.append(pl.BlockSpec((tm, d), row))
        out_shape.append(jax.ShapeDtypeStruct((m, d), BF16))
    out_specs += [pl.BlockSpec((n_seq, CONV_W - 1, tf), conv_map)] * 2
    out_shape += [jax.ShapeDtypeStruct((n_seqs * tiles_per_seq, CONV_W - 1, d_ff), F32)] * 2
    res = pl.pallas_call(
        functools.partial(_ffn_kernel, n_seq=n_seq, has_state=has_state, emit_h=emit_h, tiles_per_seq=tiles_per_seq),
        grid=(m // tm, nf),
        in_specs=in_specs,
        out_specs=out_specs,
        out_shape=out_shape,
        scratch_shapes=[pltpu.VMEM((tm, d), F32)] + scratch,
        compiler_params=_cparams("arbitrary", "arbitrary"),
        name="conv_ffn_state" if has_state else "conv_ffn",
    )(*args)
    x_new = res[0]
    h_new = res[1] if emit_h else None
    conv_state = jnp.concatenate(res[-2:], axis=-1)[tiles_per_seq - 1::tiles_per_seq]
    return x_new, h_new, conv_state


def _ret_kernel(*refs, has_s0):
    if has_s0:
        lg_ref, q_ref, k_ref, v_ref, g_ref, s0_ref, y_ref, so_ref, s_ref = refs
    else:
        lg_ref, q_ref, k_ref, v_ref, g_ref, y_ref, so_ref, s_ref = refs
    c = pl.program_id(2)
    cl = q_ref.shape[0]

    @pl.when(c == 0)
    def _():
        s_ref[...] = s0_ref[0, 0] if has_s0 else jnp.zeros_like(s_ref)

    lg = lg_ref[0, 0:1, 0:1]
    q, k, v = q_ref[...], k_ref[...], v_ref[...]
    row = lax.broadcasted_iota(jnp.int32, (cl, cl), 0)
    col = lax.broadcasted_iota(jnp.int32, (cl, cl), 1)
    diff = (row - col).astype(F32)
    decay = jnp.where(diff >= 0, jnp.exp(jnp.maximum(diff, 0.0) * lg), 0.0)
    scores = _dot_nt(q.astype(BF16), k.astype(BF16)) * decay
    o = _dot(scores.astype(BF16), v)
    idx = lax.broadcasted_iota(jnp.int32, (cl, 1), 0).astype(F32)
    s_prev = s_ref[...]
    o = o + _dot((q * jnp.exp((idx + 1.0) * lg)).astype(BF16), s_prev.astype(BF16))
    k_dec = k * jnp.exp((cl - 1.0 - idx) * lg)
    s_new = jnp.exp(cl * lg) * s_prev + _dot(k_dec.T.astype(BF16), v)
    s_ref[...] = s_new
    mu = jnp.mean(o, axis=-1, keepdims=True)
    var = jnp.mean(jnp.square(o - mu), axis=-1, keepdims=True)
    of = (o - mu) * lax.rsqrt(var + GN_EPS)
    g = g_ref[...]
    y_ref[...] = (g * jax.nn.sigmoid(g) * of).astype(y_ref.dtype)

    @pl.when(c == pl.num_programs(2) - 1)
    def _():
        so_ref[0, 0] = s_new


def _retention(qk, v, g, batch, heads, s0=None, chunk=256):
    m = qk.shape[0]
    dk, dv = qk.shape[1] // (2 * heads), v.shape[1] // heads
    seq = m // batch
    cl = _tile(seq, chunk)
    nc = seq // cl
    lg = jnp.log1p(-jnp.exp2(-5.0 - jnp.arange(heads, dtype=F32)))
    lg_tab = jnp.broadcast_to(lg[:, None, None], (heads, 8, LANES))
    tok = lambda b, h, c: (b * nc + c, h)
    in_specs = [pl.BlockSpec((1, 8, LANES), lambda b, h, c: (h, 0, 0)),
                pl.BlockSpec((cl, dk), tok), pl.BlockSpec((cl, dk), lambda b, h, c: (b * nc + c, heads + h)),
                pl.BlockSpec((cl, dv), tok), pl.BlockSpec((cl, dv), tok)]
    args = [lg_tab, qk, qk, v, g]
    if s0 is not None:
        in_specs.append(pl.BlockSpec((1, 1, dk, dv), lambda b, h, c: (b, h, 0, 0)))
        args.append(s0)
    return pl.pallas_call(
        functools.partial(_ret_kernel, has_s0=s0 is not None),
        grid=(batch, heads, nc),
        in_specs=in_specs,
        out_specs=[pl.BlockSpec((cl, dv), tok), pl.BlockSpec((1, 1, dk, dv), lambda b, h, c: (b, h, 0, 0))],
        out_shape=[jax.ShapeDtypeStruct((m, heads * dv), BF16), jax.ShapeDtypeStruct((batch, heads, dk, dv), F32)],
        scratch_shapes=[pltpu.VMEM((dk, dv), F32)],
        compiler_params=_cparams("parallel", "parallel", "arbitrary"),
        name="retention",
    )(*args)


def _mla_down_kernel(h_ref, wq_ref, wkv_ref, wr_ref, wrr_ref, gq_ref, gkv_ref, cos_ref, sin_ref,
                     cq_ref, ckv_ref, kr_ref):
    h = h_ref[...]
    cq_ref[...] = _rms(_dot(h, wq_ref[...]), gq_ref[...]).astype(cq_ref.dtype)
    ckv_ref[...] = _rms(_dot(h, wkv_ref[...]), gkv_ref[...])
    kr_ref[...] = _dot(h, wr_ref[...]) * cos_ref[...] + _dot(h, wrr_ref[...]) * sin_ref[...]


def _mla_down(h, wq, wkv, wr, wrr, gq, gkv, cos, sin, tm=512):
    m, d = h.shape
    ql, kvl, rd = wq.shape[1], wkv.shape[1], wr.shape[1]
    ltab = cos.shape[0]
    tm = _tile(math.gcd(m, ltab), tm)
    ntab = ltab // tm
    const = lambda i: (0, 0)
    row = lambda i: (i, 0)
    tab = lambda i: (i % ntab, 0)
    return pl.pallas_call(
        _mla_down_kernel,
        grid=(m // tm,),
        in_specs=[pl.BlockSpec((tm, d), row), pl.BlockSpec((d, ql), const), pl.BlockSpec((d, kvl), const),
                  pl.BlockSpec((d, rd), const), pl.BlockSpec((d, rd), const),
                  pl.BlockSpec((1, ql), const), pl.BlockSpec((1, kvl), const),
                  pl.BlockSpec((tm, rd), tab), pl.BlockSpec((tm, rd), tab)],
        out_specs=[pl.BlockSpec((tm, ql), row), pl.BlockSpec((tm, kvl), row), pl.BlockSpec((tm, rd), row)],
        out_shape=[jax.ShapeDtypeStruct((m, ql), BF16), jax.ShapeDtypeStruct((m, kvl), F32),
                   jax.ShapeDtypeStruct((m, rd), F32)],
        compiler_params=_cparams("parallel"),
        name="mla_down",
    )(h, wq, wkv, wr, wrr, gq.reshape(1, ql), gkv.reshape(1, kvl), cos, sin)


def _mla_q_kernel(cq_ref, wn_ref, wr_ref, wrr_ref, cos_ref, sin_ref, qn_ref, qr_ref):
    cq = cq_ref[...]
    qn = _dot(cq, wn_ref[...])
    for hh in range(qn_ref.shape[0]):
        qn_ref[hh] = qn[:, hh * LANES:(hh + 1) * LANES].astype(qn_ref.dtype)
    a, b = _dot(cq, wr_ref[...]), _dot(cq, wrr_ref[...])
    cos, sin = cos_ref[...], sin_ref[...]
    rd = qr_ref.shape[2]
    per = LANES // rd
    for p in range(a.shape[1] // LANES):
        r = a[:, p * LANES:(p + 1) * LANES] * cos + b[:, p * LANES:(p + 1) * LANES] * sin
        for s in range(per):
            qr_ref[p * per + s] = r[:, s * rd:(s + 1) * rd].astype(qr_ref.dtype)


def _mla_q(cq, wn, wr, wrr, cos, sin, heads, tm=512):
    m, ql = cq.shape
    nope, rd = wn.shape[1] // heads, wr.shape[1] // heads
    assert nope == LANES and LANES % rd == 0
    ltab = cos.shape[0]
    tm = _tile(math.gcd(m, ltab), tm)
    ntab = ltab // tm
    const = lambda i: (0, 0)
    tab = lambda i: (i % ntab, 0)
    return pl.pallas_call(
        _mla_q_kernel,
        grid=(m // tm,),
        in_specs=[pl.BlockSpec((tm, ql), lambda i: (i, 0)), pl.BlockSpec(wn.shape, const),
                  pl.BlockSpec(wr.shape, const), pl.BlockSpec(wrr.shape, const),
                  pl.BlockSpec((tm, LANES), tab), pl.BlockSpec((tm, LANES), tab)],
        out_specs=[pl.BlockSpec((heads, tm, nope), lambda i: (0, i, 0)),
                   pl.BlockSpec((heads, tm, rd), lambda i: (0, i, 0))],
        out_shape=[jax.ShapeDtypeStruct((heads, m, nope), BF16), jax.ShapeDtypeStruct((heads, m, rd), BF16)],
        compiler_params=_cparams("parallel"),
        name="mla_q",
    )(cq, wn, wr, wrr, cos, sin)


def _mla_flash_kernel(qn_ref, qr_ref, kn_ref, kr_ref, v_ref, o_ref, m_ref, l_ref, acc_ref, *, scale):
    qi, ki = pl.program_id(1), pl.program_id(2)
    nh, tq, vd = acc_ref.shape
    tk = kn_ref.shape[1]

    @pl.when(ki == 0)
    def _():
        m_ref[...] = jnp.full_like(m_ref, -jnp.inf)
        l_ref[...] = jnp.zeros_like(l_ref)
        acc_ref[...] = jnp.zeros_like(acc_ref)

    @pl.when(ki <= qi)
    def _():
        row_chunk = lax.broadcasted_iota(jnp.int32, (tq, tk), 0) // CHUNK
        col_chunk = lax.broadcasted_iota(jnp.int32, (tq, tk), 1) // CHUNK
        mask = (col_chunk <= row_chunk) | (ki < qi)
        kr = kr_ref[...].astype(BF16)

        def head(hh, carry):
            s = (_dot_nt(qn_ref[hh], kn_ref[hh]) + _dot_nt(qr_ref[hh], kr)) * scale
            s = jnp.where(mask, s, -jnp.inf)
            m_prev = m_ref[hh]
            m_new = jnp.maximum(m_prev, jnp.max(s, axis=-1, keepdims=True))
            alpha = jnp.exp(m_prev - m_new)
            p = jnp.exp(s - m_new)
            l_ref[hh] = alpha * l_ref[hh] + jnp.sum(p, axis=-1, keepdims=True)
            acc_ref[hh] = alpha * acc_ref[hh] + _dot(p.astype(BF16), v_ref[hh])
            m_ref[hh] = m_new
            return carry

        lax.fori_loop(0, nh, head, 0)

    @pl.when(ki == qi)
    def _():
        for hh in range(nh):
            o_ref[:, hh * vd:(hh + 1) * vd] = (acc_ref[hh] / l_ref[hh]).astype(o_ref.dtype)


def _mla_flash(qn, qr, kn, kr, v, batch, scale, t=512):
    heads, m, nope = qn.shape
    rd, vd = qr.shape[2], v.shape[2]
    seq = m // batch
    t = _tile(seq, t)
    assert t % CHUNK == 0
    nq = seq // t
    qmap = lambda b, qi, ki: (0, b * nq + qi, 0)
    kmap = lambda b, qi, ki: (0, b * nq + jnp.minimum(ki, qi), 0)
    return pl.pallas_call(
        functools.partial(_mla_flash_kernel, scale=scale),
        grid=(batch, nq, nq),
        in_specs=[pl.BlockSpec((heads, t, nope), qmap), pl.BlockSpec((heads, t, rd), qmap),
                  pl.BlockSpec((heads, t, nope), kmap),
                  pl.BlockSpec((t, rd), lambda b, qi, ki: (b * nq + jnp.minimum(ki, qi), 0)),
                  pl.BlockSpec((heads, t, vd), kmap)],
        out_specs=pl.BlockSpec((t, heads * vd), lambda b, qi, ki: (b * nq + qi, 0)),
        out_shape=jax.ShapeDtypeStruct((m, heads * vd), BF16),
        scratch_shapes=[pltpu.VMEM((heads, t, 1), F32), pltpu.VMEM((heads, t, 1), F32),
                        pltpu.VMEM((heads, t, vd), F32)],
        compiler_params=_cparams("parallel", "parallel", "arbitrary"),
        name="mla_flash",
    )(qn, qr, kn, kr, v)


def _mla_sample_kernel(qn_ref, qr_ref, ckv_ref, kr_ref, cckv_ref, ckr_ref, wuk_ref, wuv_ref, o_ref, ql_ref, *, scale):
    nh, t, nope = qn_ref.shape
    kvl = ckv_ref.shape[1]
    vd = wuv_ref.shape[2]
    for hh in range(nh):
        ql_ref[hh] = _dot_nt(qn_ref[hh], wuk_ref[hh]).astype(BF16)
    q_lat = ql_ref[...].reshape(nh * t, kvl)
    q_rope = qr_ref[...].reshape(nh * t, qr_ref.shape[2])
    c_old, r_old = cckv_ref[0].astype(BF16), ckr_ref[0].astype(BF16)
    c_new, r_new = ckv_ref[...].astype(BF16), kr_ref[...].astype(BF16)
    s_old = (_dot_nt(q_lat, c_old) + _dot_nt(q_rope, r_old)) * scale
    s_new = (_dot_nt(q_lat, c_new) + _dot_nt(q_rope, r_new)) * scale
    mx = jnp.maximum(jnp.max(s_old, axis=-1, keepdims=True), jnp.max(s_new, axis=-1, keepdims=True))
    p_old, p_new = jnp.exp(s_old - mx), jnp.exp(s_new - mx)
    denom = jnp.sum(p_old, axis=-1, keepdims=True) + jnp.sum(p_new, axis=-1, keepdims=True)
    o_lat = (_dot(p_old.astype(BF16), c_old) + _dot(p_new.astype(BF16), c_new)) / denom
    o_lat = o_lat.astype(BF16)
    for hh in range(nh):
        o_ref[:, hh * vd:(hh + 1) * vd] = _dot(o_lat[hh * t:(hh + 1) * t], wuv_ref[hh]).astype(o_ref.dtype)


def _mla_sample(qn, qr, ckv, kr, cache_ckv, cache_kr, wuk_t, wuv_t, scale):
    heads, m, nope = qn.shape
    batch, past, kvl = cache_ckv.shape
    t = m // batch
    rd = qr.shape[2]
    vd = wuv_t.shape[2]
    const3 = lambda b: (0, 0, 0)
    return pl.pallas_call(
        functools.partial(_mla_sample_kernel, scale=scale),
        grid=(batch,),
        in_specs=[pl.BlockSpec((heads, t, nope), lambda b: (0, b, 0)), pl.BlockSpec((heads, t, rd), lambda b: (0, b, 0)),
                  pl.BlockSpec((t, kvl), lambda b: (b, 0)), pl.BlockSpec((t, rd), lambda b: (b, 0)),
                  pl.BlockSpec((1, past, kvl), lambda b: (b, 0, 0)), pl.BlockSpec((1, past, rd), lambda b: (b, 0, 0)),
                  pl.BlockSpec(wuk_t.shape, const3), pl.BlockSpec(wuv_t.shape, const3)],
        out_specs=pl.BlockSpec((t, heads * vd), lambda b: (b, 0)),
        out_shape=jax.ShapeDtypeStruct((m, heads * vd), BF16),
        scratch_shapes=[pltpu.VMEM((heads, t, kvl), BF16)],
        compiler_params=_cparams("parallel"),
        name="mla_sample",
    )(qn, qr, ckv, kr, cache_ckv, cache_kr, wuk_t, wuv_t)


def _s5_kernel(*refs, has_h0, col_block):
    (x_ref, g_ref, wbr_ref, wbi_ref, ar_ref, ai_ref, cr_ref, ci_ref, d_ref), refs = refs[:9], refs[9:]
    if has_h0:
        (h0r_ref, h0i_ref), refs = refs[:2], refs[2:]
    y_ref, sr_ref, si_ref, br_ref, bi_ref, hr_ref, hi_ref = refs
    tt = pl.program_id(1)
    t, d = x_ref.shape
    n_state = br_ref.shape[1]
    n_gb, gin, gst = wbr_ref.shape

    @pl.when(tt == 0)
    def _():
        if has_h0:
            hr_ref[...] = h0r_ref[0]
            hi_ref[...] = h0i_ref[0]
        else:
            hr_ref[...] = jnp.zeros_like(hr_ref)
            hi_ref[...] = jnp.zeros_like(hi_ref)

    h = _rms(x_ref[...], g_ref[...])
    ub = h.astype(BF16)
    for gb in range(n_gb):
        u_blk = ub[:, gb * gin:(gb + 1) * gin]
        br_ref[:, gb * gst:(gb + 1) * gst] = _dot(u_blk, wbr_ref[gb])
        bi_ref[:, gb * gst:(gb + 1) * gst] = _dot(u_blk, wbi_ref[gb])

    for cb in range(n_state // col_block):
        sl = slice(cb * col_block, (cb + 1) * col_block)
        a_re, a_im = ar_ref[:, sl], ai_ref[:, sl]

        def step(s, carry):
            h_re, h_im = carry
            n_re = a_re * h_re - a_im * h_im + br_ref[pl.ds(s, 1), sl]
            n_im = a_re * h_im + a_im * h_re + bi_ref[pl.ds(s, 1), sl]
            br_ref[pl.ds(s, 1), sl] = n_re
            bi_ref[pl.ds(s, 1), sl] = n_im
            return n_re, n_im

        h_re, h_im = lax.fori_loop(0, t, step, (hr_ref[:, sl], hi_ref[:, sl]))
        hr_ref[:, sl] = h_re
        hi_ref[:, sl] = h_im

    for gb in range(n_gb):
        hs_re = br_ref[:, gb * gst:(gb + 1) * gst].astype(BF16)
        hs_im = bi_ref[:, gb * gst:(gb + 1) * gst].astype(BF16)
        cs = slice(gb * gin, (gb + 1) * gin)
        y = _dot(hs_re, cr_ref[gb]) - _dot(hs_im, ci_ref[gb]) + d_ref[:, cs] * h[:, cs]
        y_ref[:, cs] = _gelu_tanh(y).astype(y_ref.dtype)

    @pl.when(tt == pl.num_programs(1) - 1)
    def _():
        sr_ref[0] = hr_ref[...]
        si_ref[0] = hi_ref[...]


def _s5(x, g, wb_re, wb_im, a_re, a_im, wc_re, wc_im, d_skip, batch, h0=None, t=256, col_block=1024):
    m, d = x.shape
    seq = m // batch
    t = _tile(seq, t)
    nt = seq // t
    n_state = a_re.shape[1]
    col_block = _tile(n_state, col_block)
    const2 = lambda b, tt: (0, 0)
    const3 = lambda b, tt: (0, 0, 0)
    in_specs = [pl.BlockSpec((t, d), lambda b, tt: (b * nt + tt, 0)), pl.BlockSpec((1, d), const2),
                pl.BlockSpec(wb_re.shape, const3), pl.BlockSpec(wb_im.shape, const3),
                pl.BlockSpec((1, n_state), const2), pl.BlockSpec((1, n_state), const2),
                pl.BlockSpec(wc_re.shape, const3), pl.BlockSpec(wc_im.shape, const3),
                pl.BlockSpec((1, d), const2)]
    args = [x, g.reshape(1, d), wb_re, wb_im, a_re, a_im, wc_re, wc_im, d_skip.reshape(1, d)]
    if h0 is not None:
        in_specs += [pl.BlockSpec((1, 1, n_state), lambda b, tt: (b, 0, 0))] * 2
        args += [h0[0].reshape(batch, 1, n_state), h0[1].reshape(batch, 1, n_state)]
    state_spec = pl.BlockSpec((1, 1, n_state), lambda b, tt: (b, 0, 0))
    return pl.pallas_call(
        functools.partial(_s5_kernel, has_h0=h0 is not None, col_block=col_block),
        grid=(batch, nt),
        in_specs=in_specs,
        out_specs=[pl.BlockSpec((t, d), lambda b, tt: (b * nt + tt, 0)), state_spec, state_spec],
        out_shape=[jax.ShapeDtypeStruct((m, d), BF16), jax.ShapeDtypeStruct((batch, 1, n_state), F32),
                   jax.ShapeDtypeStruct((batch, 1, n_state), F32)],
        scratch_shapes=[pltpu.VMEM((t, n_state), F32), pltpu.VMEM((t, n_state), F32),
                        pltpu.VMEM((1, n_state), F32), pltpu.VMEM((1, n_state), F32)],
        compiler_params=_cparams("parallel", "arbitrary"),
        name="s5_scan",
    )(*args)


def _s5_params(a_re, a_im, log_dt, b_re, b_im, c_re, c_im, groups_per_block=8):
    g, p, gc = b_re.shape
    step = jnp.exp(log_dt)[:, None]
    xr, xi = a_re * step, a_im * step
    ab_re, ab_im = jnp.exp(xr) * jnp.cos(xi), jnp.exp(xr) * jnp.sin(xi)
    m1_re = jnp.expm1(xr) * jnp.cos(xi) - 2.0 * jnp.square(jnp.sin(0.5 * xi))
    den = a_re * a_re + a_im * a_im
    f_re = (m1_re * a_re + ab_im * a_im) / den
    f_im = (ab_im * a_re - m1_re * a_im) / den
    bb_re = f_re[..., None] * b_re - f_im[..., None] * b_im
    bb_im = f_re[..., None] * b_im + f_im[..., None] * b_re
    gpb = groups_per_block
    eye = jnp.eye(gpb, dtype=F32)

    def blockdiag_b(w):
        w = w.reshape(g // gpb, gpb, p, gc)
        return jnp.einsum('bgpc,gh->bgchp', w, eye).reshape(g // gpb, gpb * gc, gpb * p).astype(BF16)

    def blockdiag_c(w):
        w = w.reshape(g // gpb, gpb, gc, p)
        return jnp.einsum('bgcp,gh->bgphc', w, eye).reshape(g // gpb, gpb * p, gpb * gc).astype(BF16)

    return (blockdiag_b(bb_re), blockdiag_b(bb_im), ab_re.reshape(1, g * p), ab_im.reshape(1, g * p),
            blockdiag_c(c_re), blockdiag_c(c_im))


def _sb_block(q, k, v, mask, tri, r_prev, scale):
    z = _dot_nt(q, k) * scale
    sp = _softplus(z)
    if mask is not None:
        sp = jnp.where(mask, sp, 0.0)
    hi = sp.astype(BF16)
    lo = (sp - hi.astype(F32)).astype(BF16)
    csum = _dot(hi, tri) + _dot(lo, tri)
    log_w = z - csum - r_prev
    if mask is not None:
        log_w = jnp.where(mask, log_w, -jnp.inf)
    w = jnp.exp(log_w)
    return _dot(w.astype(BF16), v), r_prev + csum[:, 0:1]


def _tri(n):
    row = lax.broadcasted_iota(jnp.int32, (n, n), 0)
    col = lax.broadcasted_iota(jnp.int32, (n, n), 1)
    return (row >= col).astype(BF16)


def _sb_flash_kernel(q_ref, k_ref, v_ref, o_ref, r_ref, acc_ref, *, scale):
    qi, kj = pl.program_id(1), pl.program_id(2)
    nh, t, hd = acc_ref.shape

    @pl.when(kj == 0)
    def _():
        r_ref[...] = jnp.zeros_like(r_ref)
        acc_ref[...] = jnp.zeros_like(acc_ref)

    @pl.when(kj <= qi)
    def _():
        row = lax.broadcasted_iota(jnp.int32, (t, t), 0)
        col = lax.broadcasted_iota(jnp.int32, (t, t), 1)
        mask = (col < row) | (kj > 0)
        tri = _tri(t)

        def head(hh, carry):
            o, r = _sb_block(q_ref[hh], k_ref[hh], v_ref[hh], mask, tri, r_ref[hh], scale)
            acc_ref[hh] += o
            r_ref[hh] = r
            return carry

        lax.fori_loop(0, nh, head, 0)

    @pl.when(kj == qi)
    def _():
        for hh in range(nh):
            o_ref[:, hh * hd:(hh + 1) * hd] = acc_ref[hh].astype(o_ref.dtype)


def _sb_flash(q, k, v, batch, scale, t=256):
    heads, m, hd = q.shape
    seq = m // batch
    t = _tile(seq, t)
    nq = seq // t
    qmap = lambda b, qi, kj: (0, b * nq + qi, 0)
    kmap = lambda b, qi, kj: (0, b * nq + jnp.maximum(qi - kj, 0), 0)
    return pl.pallas_call(
        functools.partial(_sb_flash_kernel, scale=scale),
        grid=(batch, nq, nq),
        in_specs=[pl.BlockSpec((heads, t, hd), qmap), pl.BlockSpec((heads, t, hd), kmap),
                  pl.BlockSpec((heads, t, hd), kmap)],
        out_specs=pl.BlockSpec((t, heads * hd), lambda b, qi, kj: (b * nq + qi, 0)),
        out_shape=jax.ShapeDtypeStruct((m, heads * hd), BF16),
        scratch_shapes=[pltpu.VMEM((heads, t, 1), F32), pltpu.VMEM((heads, t, hd), F32)],
        compiler_params=_cparams("parallel", "parallel", "arbitrary"),
        name="sb_flash",
    )(q, k, v)


def _sb_sample_kernel(q_ref, kn_ref, vn_ref, kc_ref, vc_ref, o_ref, *, scale, blk):
    q = q_ref[0]
    t = q.shape[0]
    past = kc_ref.shape[1]
    row = lax.broadcasted_iota(jnp.int32, (t, t), 0)
    col = lax.broadcasted_iota(jnp.int32, (t, t), 1)
    acc, r = _sb_block(q, kn_ref[0], vn_ref[0], col < row, _tri(t), jnp.zeros((t, 1), F32), scale)
    tri = _tri(blk)
    for kb in range(past // blk - 1, -1, -1):
        kc = kc_ref[0, kb * blk:(kb + 1) * blk, :].astype(BF16)
        vc = vc_ref[0, kb * blk:(kb + 1) * blk, :].astype(BF16)
        o, r = _sb_block(q, kc, vc, None, tri, r, scale)
        acc = acc + o
    o_ref[...] = acc.astype(o_ref.dtype)


def _sb_sample(q, k_new, v_new, cache_k, cache_v, scale, blk=256):
    heads, m, hd = q.shape
    batch, past, _ = cache_k.shape
    t = m // batch
    blk = _tile(past, blk)
    new = lambda b, h: (h, b, 0)
    old = lambda b, h: (b, 0, h)
    return pl.pallas_call(
        functools.partial(_sb_sample_kernel, scale=scale, blk=blk),
        grid=(batch, heads),
        in_specs=[pl.BlockSpec((1, t, hd), new), pl.BlockSpec((1, t, hd), new), pl.BlockSpec((1, t, hd), new),
                  pl.BlockSpec((1, past, hd), old), pl.BlockSpec((1, past, hd), old)],
        out_specs=pl.BlockSpec((t, hd), lambda b, h: (b, h)),
        out_shape=jax.ShapeDtypeStruct((m, heads * hd), BF16),
        compiler_params=_cparams("parallel", "parallel"),
        name="sb_sample",
    )(q, k_new, v_new, cache_k, cache_v)


def _rope_tables(pos, d, reps):
    half = d // 2
    inv = ROPE_BASE ** (-jnp.arange(half, dtype=F32) * 2.0 / d)
    ang = pos.astype(F32)[:, None] * inv[None, :]
    return jnp.tile(jnp.cos(ang), (1, reps)), jnp.tile(jnp.sin(ang), (1, reps))


def _rot_half_cols(w, d):
    k, n = w.shape
    w = w.reshape(k, n // d, d)
    half = d // 2
    return jnp.concatenate([-w[..., half:], w[..., :half]], axis=-1).reshape(k, n)


def kernel(x_prompt, x_sample, state_ret, cache_mla_ckv, cache_mla_krope, state_s5_re, state_s5_im, cache_sb_k, cache_sb_v, state_ffn_conv, norm_gains, ret_w_in, ret_w_out, mla_w_down, mla_g_q, mla_g_kv, mla_w_uq, mla_w_uk, mla_w_uv, mla_w_o, s5_a_re, s5_a_im, s5_log_dt, s5_b_re, s5_b_im, s5_c_re, s5_c_im, s5_d, s5_w_glu, s5_b_glu, sb_w_qkv, sb_w_o, ffn_w_up, ffn_conv_w, ffn_conv_b, ffn_w_down):
    n_p, seq_p, d = x_prompt.shape
    n_s, seq_s, _ = x_sample.shape
    depth = norm_gains.shape[0]
    n_mixers = 4
    streams = (dict(batch=n_p, seq=seq_p, pos=jnp.arange(seq_p)),
               dict(batch=n_s, seq=seq_s, pos=None))
    xs = [x_prompt.reshape(n_p * seq_p, d), x_sample.reshape(n_s * seq_s, d)]
    hs = [_rmsnorm(x, norm_gains[0, 0]) for x in xs]
    outs = {name: ([], []) for name in ('ret', 'ckv', 'kr', 's5re', 's5im', 'sbk', 'sbv', 'conv')}

    past_len = cache_mla_ckv.shape[2]

    def positions(si):
        if si == 0:
            return jnp.arange(seq_p)
        return jnp.tile(past_len + jnp.arange(seq_s), n_s)

    for i in range(depth):
        j, kind = i // n_mixers, i % n_mixers
        g = norm_gains[i]
        g_next = norm_gains[i + 1, 0] if i + 1 < depth else g[0]
        emit_next = i + 1 < depth and (i + 1) % n_mixers != 2
        for si, st in enumerate(streams):
            x, h, batch = xs[si], hs[si], st['batch']
            if kind == 0:
                heads, dk, dv = state_ret.shape[2], state_ret.shape[3], state_ret.shape[4]
                qk_w, v_w = heads * dk, heads * dv
                w_in = ret_w_in[j].astype(BF16)
                cos, sin = _rope_tables(positions(si), dk, 1)
                qk = _linear_rope(h, w_in, 2 * qk_w, cos, sin, dk, qk_w, dk ** -0.5)
                v, = _linear(h, w_in, 2 * qk_w, v_w, (BF16,), name="ret_v")
                gate, = _linear(h, w_in, 2 * qk_w + v_w, v_w, (F32,), name="ret_g")
                s0 = state_ret[j] if si == 1 else None
                y, s1 = _retention(qk, v, gate, batch, heads, s0)
                outs['ret'][si].append(s1)
                x, h = _outproj(y, ret_w_out[j].astype(BF16), x, g[1], g[2], name="ret_outproj")
            elif kind == 1:
                ql, kvl = mla_g_q.shape[1], mla_g_kv.shape[1]
                heads, nope = mla_w_uk.shape[2], mla_w_uk.shape[3]
                rd, vd = cache_mla_krope.shape[3], mla_w_uv.shape[3]
                past = cache_mla_ckv.shape[2]
                scale = (nope + rd) ** -0.5
                w_down = mla_w_down[j]
                w_kr = w_down[:, ql + kvl:]
                pos = positions(si)
                cos_k, sin_k = _rope_tables(pos, rd, 2)
                cos_q, sin_q = _rope_tables(pos, rd, 2 * LANES // rd)
                cq, ckv, kr = _mla_down(h, w_down[:, :ql].astype(BF16), w_down[:, ql:ql + kvl].astype(BF16),
                                        w_kr.astype(BF16), _rot_half_cols(w_kr, rd).astype(BF16),
                                        mla_g_q[j], mla_g_kv[j], cos_k, sin_k)
                w_uq = mla_w_uq[j].reshape(ql, heads, nope + rd)
                w_qn = w_uq[:, :, :nope].reshape(ql, heads * nope)
                w_qr = w_uq[:, :, nope:].reshape(ql, heads * rd)
                qn, qr = _mla_q(cq, w_qn.astype(BF16), w_qr.astype(BF16), _rot_half_cols(w_qr, rd).astype(BF16),
                                cos_q, sin_q, heads)
                outs['ckv'][si].append(ckv.reshape(batch, st['seq'], kvl))
                outs['kr'][si].append(kr.reshape(batch, st['seq'], rd))
                if si == 0:
                    kn, = _linear(ckv, mla_w_uk[j].reshape(kvl, heads * nope).astype(BF16), 0, heads * nope,
                                  (BF16,), (True,), name="mla_k_nope")
                    v, = _linear(ckv, mla_w_uv[j].reshape(kvl, heads * vd).astype(BF16), 0, heads * vd,
                                 (BF16,), (True,), name="mla_v")
                    o = _mla_flash(qn, qr, kn, kr, v, batch, scale)
                else:
                    o = _mla_sample(qn, qr, ckv, kr, cache_mla_ckv[j], cache_mla_krope[j],
                                    jnp.transpose(mla_w_uk[j], (1, 0, 2)).astype(BF16),
                                    jnp.transpose(mla_w_uv[j], (1, 0, 2)).astype(BF16), scale)
                x, h = _outproj(o, mla_w_o[j].astype(BF16), x, g[1], g[2], name="mla_outproj")
            elif kind == 2:
                params = _s5_params(s5_a_re[j], s5_a_im[j], s5_log_dt[j], s5_b_re[j], s5_b_im[j],
                                    s5_c_re[j], s5_c_im[j])
                h0 = (state_s5_re[j], state_s5_im[j]) if si == 1 else None
                y, s_re, s_im = _s5(x, g[0], *params, s5_d[j], batch, h0)
                outs['s5re'][si].append(s_re.reshape(batch, *s5_a_re.shape[1:]))
                outs['s5im'][si].append(s_im.reshape(batch, *s5_a_re.shape[1:]))
                x, h = _glu_outproj(y, s5_w_glu[j].astype(BF16), s5_b_glu[j], x, g[1], g[2])
            else:
                heads, hd = cache_sb_k.shape[3], cache_sb_k.shape[4]
                past = cache_sb_k.shape[2]
                w_qkv = sb_w_qkv[j].astype(BF16)
                scale = hd ** -0.5
                q, = _linear(h, w_qkv, 0, heads * hd, (BF16,), (True,), name="sb_q")
                k32, k16 = _linear(h, w_qkv, heads * hd, heads * hd, (F32, BF16), (False, True), name="sb_k")
                v32, v16 = _linear(h, w_qkv, 2 * heads * hd, heads * hd, (F32, BF16), (False, True), name="sb_v")
                outs['sbk'][si].append(k32.reshape(batch, st['seq'], heads, hd))
                outs['sbv'][si].append(v32.reshape(batch, st['seq'], heads, hd))
                if si == 0:
                    o = _sb_flash(q, k16, v16, batch, scale)
                else:
                    o = _sb_sample(q, k16, v16, cache_sb_k[j].reshape(batch, past, heads * hd),
                                   cache_sb_v[j].reshape(batch, past, heads * hd), scale)
                x, h = _outproj(o, sb_w_o[j].astype(BF16), x, g[1], g[2], name="sb_outproj")
            x, h, conv = _conv_ffn(h, x, ffn_w_up[i].astype(BF16), ffn_conv_w[i], ffn_conv_b[i],
                                   ffn_w_down[i].astype(BF16), g[3], g_next, st['seq'],
                                   state=state_ffn_conv[i] if si == 1 else None, emit_h=emit_next)
            outs['conv'][si].append(conv)
            xs[si], hs[si] = x, h

    stack = lambda name: (jnp.stack(outs[name][0]), jnp.stack(outs[name][1]))
    return (xs[0].reshape(x_prompt.shape), xs[1].reshape(x_sample.shape),
            *stack('ret'), *stack('ckv'), *stack('kr'), *stack('s5re'), *stack('s5im'),
            *stack('sbk'), *stack('sbv'), *stack('conv'))
```

```python
import functools
import math

import jax
import jax.numpy as jnp
from jax import lax
from jax.experimental import pallas as pl
from jax.experimental.pallas import tpu as pltpu

F32 = jnp.float32
BF16 = jnp.bfloat16
NORM_EPS = 1e-6
GN_EPS = 1e-5
ROPE_BASE = 10000.0
CHUNK = 64
CONV_W = 3
LANES = 128
VMEM_LIMIT = 56 * 1024 * 1024
SB_EXIT_SUM = 104.0


def _cparams(*sem):
    return pltpu.CompilerParams(dimension_semantics=sem, vmem_limit_bytes=VMEM_LIMIT)


def _tile(n, pref):
    t = min(n, pref)
    while n % t:
        t -= 1
    return t


def _dot(a, b):
    return jnp.dot(a, b, preferred_element_type=F32)


def _dot_nt(a, b):
    return lax.dot_general(a, b, (((1,), (1,)), ((), ())), preferred_element_type=F32)


def _rms(x, g):
    return x * lax.rsqrt(jnp.mean(x * x, axis=-1, keepdims=True) + NORM_EPS) * g


def _gelu_tanh(x):
    return 0.5 * x * (1.0 + jnp.tanh(math.sqrt(2.0 / math.pi) * (x + 0.044715 * (x * x * x))))


def _softplus(z):
    return jnp.maximum(z, 0.0) + jnp.log(1.0 + jnp.exp(-jnp.abs(z)))


def _rmsnorm_kernel(x_ref, g_ref, o_ref):
    o_ref[...] = _rms(x_ref[...], g_ref[...]).astype(o_ref.dtype)


def _rmsnorm(x, g, out_dtype=BF16, tm=512):
    m, d = x.shape
    tm = _tile(m, tm)
    return pl.pallas_call(
        _rmsnorm_kernel,
        grid=(m // tm,),
        in_specs=[pl.BlockSpec((tm, d), lambda i: (i, 0)), pl.BlockSpec((1, d), lambda i: (0, 0))],
        out_specs=pl.BlockSpec((tm, d), lambda i: (i, 0)),
        out_shape=jax.ShapeDtypeStruct((m, d), out_dtype),
        compiler_params=_cparams("parallel"),
        name="rmsnorm",
    )(x, g.reshape(1, d))


def _linear_kernel(a_ref, w_ref, *o_refs, head_major):
    acc = _dot(a_ref[...].astype(BF16), w_ref[...])
    for o_ref in o_refs:
        if head_major and len(o_ref.shape) == 3:
            for hh in range(o_ref.shape[0]):
                o_ref[hh] = acc[:, hh * LANES:(hh + 1) * LANES].astype(o_ref.dtype)
        else:
            o_ref[...] = acc.astype(o_ref.dtype)


def _linear(a, w, col_off, n_out, out_dtypes, head_major=(), tm=1024, tn=512, name="linear"):
    m, k = a.shape
    tm, tn = _tile(m, tm), _tile(n_out, tn)
    assert col_off % tn == 0 and tn % LANES == 0
    joff = col_off // tn
    out_specs, out_shape = [], []
    for dt, hm in zip(out_dtypes, head_major or (False,) * len(out_dtypes)):
        if hm:
            out_specs.append(pl.BlockSpec((tn // LANES, tm, LANES), lambda i, j: (j, i, 0)))
            out_shape.append(jax.ShapeDtypeStruct((n_out // LANES, m, LANES), dt))
        else:
            out_specs.append(pl.BlockSpec((tm, tn), lambda i, j: (i, j)))
            out_shape.append(jax.ShapeDtypeStruct((m, n_out), dt))
    return pl.pallas_call(
        functools.partial(_linear_kernel, head_major=bool(head_major)),
        grid=(m // tm, n_out // tn),
        in_specs=[pl.BlockSpec((tm, k), lambda i, j: (i, 0)),
                  pl.BlockSpec((k, tn), lambda i, j: (0, joff + j))],
        out_specs=out_specs,
        out_shape=out_shape,
        compiler_params=_cparams("parallel", "arbitrary"),
        name=name,
    )(a, w)


def _linear_rope_kernel(a_ref, w_ref, cos_ref, sin_ref, o_ref, *, dk, n_q_tiles, k_scale):
    acc = _dot(a_ref[...], w_ref[...])
    scale = jnp.where(pl.program_id(1) >= n_q_tiles, k_scale, 1.0).astype(F32)
    cos, sin = cos_ref[...] * scale, sin_ref[...] * scale
    half = dk // 2
    for hh in range(acc.shape[1] // dk):
        x1 = acc[:, hh * dk:hh * dk + half]
        x2 = acc[:, hh * dk + half:(hh + 1) * dk]
        o_ref[:, hh * dk:hh * dk + half] = x1 * cos - x2 * sin
        o_ref[:, hh * dk + half:(hh + 1) * dk] = x1 * sin + x2 * cos


def _linear_rope(a, w, n_out, cos, sin, dk, n_q, k_scale, tm=1024, tn=512):
    m, k = a.shape
    ltab = cos.shape[0]
    tm = _tile(math.gcd(m, ltab), tm)
    tn = _tile(n_out, tn)
    ntab = ltab // tm
    return pl.pallas_call(
        functools.partial(_linear_rope_kernel, dk=dk, n_q_tiles=n_q // tn, k_scale=k_scale),
        grid=(m // tm, n_out // tn),
        in_specs=[pl.BlockSpec((tm, k), lambda i, j: (i, 0)),
                  pl.BlockSpec((k, tn), lambda i, j: (0, j)),
                  pl.BlockSpec((tm, dk // 2), lambda i, j: (i % ntab, 0)),
                  pl.BlockSpec((tm, dk // 2), lambda i, j: (i % ntab, 0))],
        out_specs=pl.BlockSpec((tm, tn), lambda i, j: (i, j)),
        out_shape=jax.ShapeDtypeStruct((m, n_out), F32),
        compiler_params=_cparams("parallel", "arbitrary"),
        name="ret_qk_rope",
    )(a, w, cos, sin)


def _outproj_kernel(a_ref, w_ref, x_ref, gp_ref, gn_ref, xo_ref, *rest, emit_h):
    acc_ref = rest[-1]
    kk = pl.program_id(1)

    @pl.when(kk == 0)
    def _():
        acc_ref[...] = jnp.zeros_like(acc_ref)

    acc_ref[...] += _dot(a_ref[...].astype(BF16), w_ref[...])

    @pl.when(kk == pl.num_programs(1) - 1)
    def _():
        xn = x_ref[...] + _rms(acc_ref[...], gp_ref[...])
        xo_ref[...] = xn
        if emit_h:
            rest[0][...] = _rms(xn, gn_ref[...]).astype(BF16)


def _outproj(a, w, x, g_post, g_next, emit_h=True, tm=512, tk=512, name="outproj"):
    m, k = a.shape
    d = w.shape[1]
    tm, tk = _tile(m, tm), _tile(k, tk)
    out_specs = [pl.BlockSpec((tm, d), lambda i, kk: (i, 0))]
    out_shape = [jax.ShapeDtypeStruct((m, d), F32)]
    if emit_h:
        out_specs.append(pl.BlockSpec((tm, d), lambda i, kk: (i, 0)))
        out_shape.append(jax.ShapeDtypeStruct((m, d), BF16))
    res = pl.pallas_call(
        functools.partial(_outproj_kernel, emit_h=emit_h),
        grid=(m // tm, k // tk),
        in_specs=[pl.BlockSpec((tm, tk), lambda i, kk: (i, kk)),
                  pl.BlockSpec((tk, d), lambda i, kk: (kk, 0)),
                  pl.BlockSpec((tm, d), lambda i, kk: (i, 0)),
                  pl.BlockSpec((1, d), lambda i, kk: (0, 0)),
                  pl.BlockSpec((1, d), lambda i, kk: (0, 0))],
        out_specs=out_specs,
        out_shape=out_shape,
        scratch_shapes=[pltpu.VMEM((tm, d), F32)],
        compiler_params=_cparams("parallel", "arbitrary"),
        name=name,
    )(a, w, x, g_post.reshape(1, d), g_next.reshape(1, d))
    return (res[0], res[1]) if emit_h else (res[0], None)


def _glu_outproj_kernel(a_ref, wv_ref, wg_ref, bv_ref, bg_ref, x_ref, gp_ref, gn_ref, xo_ref, ho_ref,
                        accv_ref, accg_ref):
    kk = pl.program_id(1)

    @pl.when(kk == 0)
    def _():
        accv_ref[...] = jnp.zeros_like(accv_ref)
        accg_ref[...] = jnp.zeros_like(accg_ref)

    a = a_ref[...]
    accv_ref[...] += _dot(a, wv_ref[...])
    accg_ref[...] += _dot(a, wg_ref[...])

    @pl.when(kk == pl.num_programs(1) - 1)
    def _():
        mix = (accv_ref[...] + bv_ref[...]) * jax.nn.sigmoid(accg_ref[...] + bg_ref[...])
        xn = x_ref[...] + _rms(mix, gp_ref[...])
        xo_ref[...] = xn
        ho_ref[...] = _rms(xn, gn_ref[...]).astype(BF16)


def _glu_outproj(a, w, b, x, g_post, g_next, tm=512, tk=512):
    m, k = a.shape
    d = w.shape[1] // 2
    tm, tk = _tile(m, tm), _tile(k, tk)
    row = lambda i, kk: (i, 0)
    const = lambda i, kk: (0, 0)
    return pl.pallas_call(
        _glu_outproj_kernel,
        grid=(m // tm, k // tk),
        in_specs=[pl.BlockSpec((tm, tk), lambda i, kk: (i, kk)),
                  pl.BlockSpec((tk, d), lambda i, kk: (kk, 0)),
                  pl.BlockSpec((tk, d), lambda i, kk: (kk, 1)),
                  pl.BlockSpec((1, d), lambda i, kk: (0, 0)),
                  pl.BlockSpec((1, d), lambda i, kk: (0, 1)),
                  pl.BlockSpec((tm, d), row), pl.BlockSpec((1, d), const), pl.BlockSpec((1, d), const)],
        out_specs=[pl.BlockSpec((tm, d), row), pl.BlockSpec((tm, d), row)],
        out_shape=[jax.ShapeDtypeStruct((m, d), F32), jax.ShapeDtypeStruct((m, d), BF16)],
        scratch_shapes=[pltpu.VMEM((tm, d), F32), pltpu.VMEM((tm, d), F32)],
        compiler_params=_cparams("parallel", "arbitrary"),
        name="s5_glu_outproj",
    )(a, w, w, b.reshape(1, 2 * d), b.reshape(1, 2 * d), x, g_post.reshape(1, d), g_next.reshape(1, d))


def _ffn_kernel(*refs, n_seq, has_state, emit_h, tiles_per_seq, nf):
    (h_ref, x_ref, wv_ref, wg_ref, cwv_ref, cwg_ref, cbv_ref, cbg_ref, wd_ref, gp_ref, gn_ref), refs = refs[:11], refs[11:]
    if has_state:
        (stv_ref, stg_ref), refs = refs[:2], refs[2:]
    xo_ref, refs = refs[0], refs[1:]
    if emit_h:
        ho_ref, refs = refs[0], refs[1:]
    cv_ref, cg_ref, acc_ref, act_even_ref, act_odd_ref = refs[:5]
    carry_refs = refs[5:]
    i, j = pl.program_id(0), pl.program_id(1)
    tm, tf = h_ref.shape[0], wv_ref.shape[1]
    seq = tm // n_seq

    if not has_state:
        @pl.when((i == 0) & (j == 0))
        def _():
            for carry_ref in carry_refs[:2]:
                carry_ref[...] = jnp.zeros_like(carry_ref)

    def conv_with_state(w_ref, cw_ref, cb_ref, st_ref, c_out_ref):
        up = _dot(h_ref[...], w_ref[...]).reshape(n_seq, seq, tf)
        prev1, prev2 = st_ref[:, 1:2, :], st_ref[:, 0:1, :]
        pos = lax.broadcasted_iota(jnp.int32, up.shape, 1)
        p1 = jnp.where(pos == 0, prev1, pltpu.roll(up, 1, axis=1))
        p2 = jnp.where(pos == 0, prev2, jnp.where(pos == 1, prev1, pltpu.roll(up, 2, axis=1)))
        c_out_ref[...] = up[:, seq - 2:, :]
        cw = cw_ref[...]
        c = cb_ref[...][None] + cw[0:1][None] * p2 + cw[1:2][None] * p1 + cw[2:3][None] * up
        return c.reshape(tm, tf)

    def conv_with_carry(w_ref, cw_ref, cb_ref, carry_ref, ext_ref, c_out_ref):
        up = _dot(h_ref[...], w_ref[...])
        ext_ref[0:8, :] = jnp.where(i % tiles_per_seq != 0, carry_ref[j], 0.0)
        ext_ref[8:, :] = up
        carry_ref[j] = up[tm - 8:, :]
        c_out_ref[0] = up[tm - 2:, :]
        cw = cw_ref[...]
        return cb_ref[...] + cw[0:1] * ext_ref[6:6 + tm, :] + cw[1:2] * ext_ref[7:7 + tm, :] + cw[2:3] * up

    def activation():
        if has_state:
            val = conv_with_state(wv_ref, cwv_ref, cbv_ref, stv_ref, cv_ref)
            gate = conv_with_state(wg_ref, cwg_ref, cbg_ref, stg_ref, cg_ref)
        else:
            val = conv_with_carry(wv_ref, cwv_ref, cbv_ref, carry_refs[0], carry_refs[2], cv_ref)
            gate = conv_with_carry(wg_ref, cwg_ref, cbg_ref, carry_refs[1], carry_refs[3], cg_ref)
        return (_gelu_tanh(gate) * val).astype(BF16)

    @pl.when(j == 0)
    def _():
        acc_ref[...] = jnp.zeros_like(acc_ref)
        act_odd_ref[...] = jnp.zeros_like(act_odd_ref)

    def step(act_new_ref, act_old_ref):
        act_new_ref[...] = activation()
        acc_ref[...] += _dot(act_old_ref[...], wd_ref[...])

    @pl.when((j < nf) & (j % 2 == 0))
    def _():
        step(act_even_ref, act_odd_ref)

    @pl.when((j < nf) & (j % 2 == 1))
    def _():
        step(act_odd_ref, act_even_ref)

    @pl.when(j == nf)
    def _():
        last = act_even_ref if (nf - 1) % 2 == 0 else act_odd_ref
        acc_ref[...] += _dot(last[...], wd_ref[...])
        xn = x_ref[...] + _rms(acc_ref[...], gp_ref[...])
        xo_ref[...] = xn
        if emit_h:
            ho_ref[...] = _rms(xn, gn_ref[...]).astype(BF16)


def _conv_ffn(h, x, w_up, conv_w, conv_b, w_down, g_post, g_next, seq_len, state=None, emit_h=True, tm=512, tf=512):
    m, d = h.shape
    d_ff = w_down.shape[0]
    tf = _tile(d_ff, tf)
    nf = d_ff // tf
    has_state = state is not None
    if has_state:
        tm = _tile(m // seq_len, max(tm // seq_len, 1)) * seq_len
        n_seq, tiles_per_seq = tm // seq_len, 1
    else:
        tm = _tile(seq_len, tm)
        n_seq, tiles_per_seq = 1, seq_len // tm
    n_seqs = m // seq_len
    row = lambda i, j: (i, 0)
    const = lambda i, j: (0, 0)
    up = lambda j: jnp.minimum(j, nf - 1)
    val_cols = lambda i, j: (0, up(j))
    gate_cols = lambda i, j: (0, nf + up(j))
    in_specs = [pl.BlockSpec((tm, d), row), pl.BlockSpec((tm, d), row),
                pl.BlockSpec((d, tf), val_cols), pl.BlockSpec((d, tf), gate_cols),
                pl.BlockSpec((CONV_W, tf), val_cols), pl.BlockSpec((CONV_W, tf), gate_cols),
                pl.BlockSpec((1, tf), val_cols), pl.BlockSpec((1, tf), gate_cols),
                pl.BlockSpec((tf, d), lambda i, j: (jnp.maximum(j - 1, 0), 0)),
                pl.BlockSpec((1, d), const), pl.BlockSpec((1, d), const)]
    args = [h, x, w_up, w_up, conv_w, conv_w, conv_b.reshape(1, -1), conv_b.reshape(1, -1), w_down,
            g_post.reshape(1, d), g_next.reshape(1, d)]
    if has_state:
        in_specs += [pl.BlockSpec((n_seq, CONV_W - 1, tf), lambda i, j: (i, 0, up(j))),
                     pl.BlockSpec((n_seq, CONV_W - 1, tf), lambda i, j: (i, 0, nf + up(j)))]
        args += [state, state]
        scratch = []
    else:
        scratch = [pltpu.VMEM((nf, 8, tf), F32), pltpu.VMEM((nf, 8, tf), F32),
                   pltpu.VMEM((tm + 8, tf), F32), pltpu.VMEM((tm + 8, tf), F32)]
    conv_map = lambda i, j: (i, 0, up(j))
    out_specs = [pl.BlockSpec((tm, d), row)]
    out_shape = [jax.ShapeDtypeStruct((m, d), F32)]
    if emit_h:
        out_specs.append(pl.BlockSpec((tm, d), row))
        out_shape.append(jax.ShapeDtypeStruct((m, d), BF16))
    out_specs += [pl.BlockSpec((n_seq, CONV_W - 1, tf), conv_map)] * 2
    out_shape += [jax.ShapeDtypeStruct((n_seqs * tiles_per_seq, CONV_W - 1, d_ff), F32)] * 2
    res = pl.pallas_call(
        functools.partial(_ffn_kernel, n_seq=n_seq, has_state=has_state, emit_h=emit_h, tiles_per_seq=tiles_per_seq,
                          nf=nf),
        grid=(m // tm, nf + 1),
        in_specs=in_specs,
        out_specs=out_specs,
        out_shape=out_shape,
        scratch_shapes=[pltpu.VMEM((tm, d), F32), pltpu.VMEM((tm, tf), BF16), pltpu.VMEM((tm, tf), BF16)] + scratch,
        compiler_params=_cparams("arbitrary", "arbitrary"),
        name="conv_ffn_state" if has_state else "conv_ffn",
    )(*args)
    x_new = res[0]
    h_new = res[1] if emit_h else None
    conv_state = jnp.concatenate(res[-2:], axis=-1)[tiles_per_seq - 1::tiles_per_seq]
    return x_new, h_new, conv_state


def _ret_kernel(*refs, has_s0):
    if has_s0:
        lg_ref, q_ref, k_ref, v_ref, g_ref, s0_ref, y_ref, so_ref, s_ref = refs
    else:
        lg_ref, q_ref, k_ref, v_ref, g_ref, y_ref, so_ref, s_ref = refs
    c = pl.program_id(2)
    cl = q_ref.shape[0]

    @pl.when(c == 0)
    def _():
        s_ref[...] = s0_ref[0, 0] if has_s0 else jnp.zeros_like(s_ref)

    lg = lg_ref[0, 0:1, 0:1]
    q, k, v = q_ref[...], k_ref[...], v_ref[...]
    row = lax.broadcasted_iota(jnp.int32, (cl, cl), 0)
    col = lax.broadcasted_iota(jnp.int32, (cl, cl), 1)
    diff = (row - col).astype(F32)
    decay = jnp.where(diff >= 0, jnp.exp(jnp.maximum(diff, 0.0) * lg), 0.0)
    scores = _dot_nt(q.astype(BF16), k.astype(BF16)) * decay
    o = _dot(scores.astype(BF16), v)
    idx = lax.broadcasted_iota(jnp.int32, (cl, 1), 0).astype(F32)
    s_prev = s_ref[...]
    o = o + _dot((q * jnp.exp((idx + 1.0) * lg)).astype(BF16), s_prev.astype(BF16))
    k_dec = k * jnp.exp((cl - 1.0 - idx) * lg)
    s_new = jnp.exp(cl * lg) * s_prev + _dot(k_dec.T.astype(BF16), v)
    s_ref[...] = s_new
    mu = jnp.mean(o, axis=-1, keepdims=True)
    var = jnp.mean(jnp.square(o - mu), axis=-1, keepdims=True)
    of = (o - mu) * lax.rsqrt(var + GN_EPS)
    g = g_ref[...]
    y_ref[...] = (g * jax.nn.sigmoid(g) * of).astype(y_ref.dtype)

    @pl.when(c == pl.num_programs(2) - 1)
    def _():
        so_ref[0, 0] = s_new


def _retention(qk, v, g, batch, heads, s0=None, chunk=256):
    m = qk.shape[0]
    dk, dv = qk.shape[1] // (2 * heads), v.shape[1] // heads
    seq = m // batch
    cl = _tile(seq, chunk)
    nc = seq // cl
    lg = jnp.log1p(-jnp.exp2(-5.0 - jnp.arange(heads, dtype=F32)))
    lg_tab = jnp.broadcast_to(lg[:, None, None], (heads, 8, LANES))
    tok = lambda b, h, c: (b * nc + c, h)
    in_specs = [pl.BlockSpec((1, 8, LANES), lambda b, h, c: (h, 0, 0)),
                pl.BlockSpec((cl, dk), tok), pl.BlockSpec((cl, dk), lambda b, h, c: (b * nc + c, heads + h)),
                pl.BlockSpec((cl, dv), tok), pl.BlockSpec((cl, dv), tok)]
    args = [lg_tab, qk, qk, v, g]
    if s0 is not None:
        in_specs.append(pl.BlockSpec((1, 1, dk, dv), lambda b, h, c: (b, h, 0, 0)))
        args.append(s0)
    return pl.pallas_call(
        functools.partial(_ret_kernel, has_s0=s0 is not None),
        grid=(batch, heads, nc),
        in_specs=in_specs,
        out_specs=[pl.BlockSpec((cl, dv), tok), pl.BlockSpec((1, 1, dk, dv), lambda b, h, c: (b, h, 0, 0))],
        out_shape=[jax.ShapeDtypeStruct((m, heads * dv), BF16), jax.ShapeDtypeStruct((batch, heads, dk, dv), F32)],
        scratch_shapes=[pltpu.VMEM((dk, dv), F32)],
        compiler_params=_cparams("parallel", "parallel", "arbitrary"),
        name="retention",
    )(*args)


def _mla_down_kernel(h_ref, wq_ref, wkv_ref, wr_ref, wrr_ref, gq_ref, gkv_ref, cos_ref, sin_ref,
                     cq_ref, ckv_ref, kr_ref):
    h = h_ref[...]
    cq_ref[...] = _rms(_dot(h, wq_ref[...]), gq_ref[...]).astype(cq_ref.dtype)
    ckv_ref[...] = _rms(_dot(h, wkv_ref[...]), gkv_ref[...])
    kr_ref[...] = _dot(h, wr_ref[...]) * cos_ref[...] + _dot(h, wrr_ref[...]) * sin_ref[...]


def _mla_down(h, wq, wkv, wr, wrr, gq, gkv, cos, sin, tm=512):
    m, d = h.shape
    ql, kvl, rd = wq.shape[1], wkv.shape[1], wr.shape[1]
    ltab = cos.shape[0]
    tm = _tile(math.gcd(m, ltab), tm)
    ntab = ltab // tm
    const = lambda i: (0, 0)
    row = lambda i: (i, 0)
    tab = lambda i: (i % ntab, 0)
    return pl.pallas_call(
        _mla_down_kernel,
        grid=(m // tm,),
        in_specs=[pl.BlockSpec((tm, d), row), pl.BlockSpec((d, ql), const), pl.BlockSpec((d, kvl), const),
                  pl.BlockSpec((d, rd), const), pl.BlockSpec((d, rd), const),
                  pl.BlockSpec((1, ql), const), pl.BlockSpec((1, kvl), const),
                  pl.BlockSpec((tm, rd), tab), pl.BlockSpec((tm, rd), tab)],
        out_specs=[pl.BlockSpec((tm, ql), row), pl.BlockSpec((tm, kvl), row), pl.BlockSpec((tm, rd), row)],
        out_shape=[jax.ShapeDtypeStruct((m, ql), BF16), jax.ShapeDtypeStruct((m, kvl), F32),
                   jax.ShapeDtypeStruct((m, rd), F32)],
        compiler_params=_cparams("parallel"),
        name="mla_down",
    )(h, wq, wkv, wr, wrr, gq.reshape(1, ql), gkv.reshape(1, kvl), cos, sin)


def _mla_q_kernel(cq_ref, wn_ref, wr_ref, wrr_ref, cos_ref, sin_ref, qn_ref, qr_ref, *, scale):
    cq = cq_ref[...]
    qn = _dot(cq, wn_ref[...]) * scale
    for hh in range(qn_ref.shape[0]):
        qn_ref[hh] = qn[:, hh * LANES:(hh + 1) * LANES].astype(qn_ref.dtype)
    a, b = _dot(cq, wr_ref[...]), _dot(cq, wrr_ref[...])
    cos, sin = cos_ref[...] * scale, sin_ref[...] * scale
    rd = qr_ref.shape[2]
    per = LANES // rd
    for p in range(a.shape[1] // LANES):
        r = a[:, p * LANES:(p + 1) * LANES] * cos + b[:, p * LANES:(p + 1) * LANES] * sin
        for s in range(per):
            qr_ref[p * per + s] = r[:, s * rd:(s + 1) * rd].astype(qr_ref.dtype)


def _mla_q(cq, wn, wr, wrr, cos, sin, heads, scale, tm=512):
    m, ql = cq.shape
    nope, rd = wn.shape[1] // heads, wr.shape[1] // heads
    assert nope == LANES and LANES % rd == 0
    ltab = cos.shape[0]
    tm = _tile(math.gcd(m, ltab), tm)
    ntab = ltab // tm
    const = lambda i: (0, 0)
    tab = lambda i: (i % ntab, 0)
    return pl.pallas_call(
        functools.partial(_mla_q_kernel, scale=scale),
        grid=(m // tm,),
        in_specs=[pl.BlockSpec((tm, ql), lambda i: (i, 0)), pl.BlockSpec(wn.shape, const),
                  pl.BlockSpec(wr.shape, const), pl.BlockSpec(wrr.shape, const),
                  pl.BlockSpec((tm, LANES), tab), pl.BlockSpec((tm, LANES), tab)],
        out_specs=[pl.BlockSpec((heads, tm, nope), lambda i: (0, i, 0)),
                   pl.BlockSpec((heads, tm, rd), lambda i: (0, i, 0))],
        out_shape=[jax.ShapeDtypeStruct((heads, m, nope), BF16), jax.ShapeDtypeStruct((heads, m, rd), BF16)],
        compiler_params=_cparams("parallel"),
        name="mla_q",
    )(cq, wn, wr, wrr, cos, sin)


def _mla_flash_kernel(qn_ref, qr_ref, kn_ref, kr_ref, v_ref, o_ref, m_ref, l_ref, acc_ref):
    qi, ki = pl.program_id(1), pl.program_id(2)
    nh, tq, vd = acc_ref.shape
    tk = kn_ref.shape[1]

    @pl.when(ki == 0)
    def _():
        m_ref[...] = jnp.full_like(m_ref, -jnp.inf)
        l_ref[...] = jnp.zeros_like(l_ref)
        acc_ref[...] = jnp.zeros_like(acc_ref)

    def sweep(diagonal):
        kr = kr_ref[...].astype(BF16)
        if diagonal:
            row_chunk = lax.broadcasted_iota(jnp.int32, (tq, tk), 0) // CHUNK
            col_chunk = lax.broadcasted_iota(jnp.int32, (tq, tk), 1) // CHUNK
            mask = col_chunk <= row_chunk

        def head(hh, carry):
            s = _dot_nt(qn_ref[hh], kn_ref[hh]) + _dot_nt(qr_ref[hh], kr)
            if diagonal:
                s = jnp.where(mask, s, -jnp.inf)
            m_prev = m_ref[hh]
            m_new = jnp.maximum(m_prev, jnp.max(s, axis=-1, keepdims=True))
            alpha = jnp.exp(m_prev - m_new)
            p = jnp.concatenate([jnp.exp(s[:, c * LANES:(c + 1) * LANES] - m_new) for c in range(tk // LANES)],
                                axis=-1)
            l_ref[hh] = alpha * l_ref[hh] + jnp.sum(p, axis=-1, keepdims=True)
            acc_ref[hh] = alpha * acc_ref[hh] + _dot(p.astype(BF16), v_ref[hh])
            m_ref[hh] = m_new
            return carry

        lax.fori_loop(0, nh, head, 0)

    @pl.when(ki < qi)
    def _():
        sweep(False)

    @pl.when(ki == qi)
    def _():
        sweep(True)
        for hh in range(nh):
            o_ref[:, hh * vd:(hh + 1) * vd] = (acc_ref[hh] / l_ref[hh]).astype(o_ref.dtype)


def _mla_flash(qn, qr, kn, kr, v, batch, t=512):
    heads, m, nope = qn.shape
    rd, vd = qr.shape[2], v.shape[2]
    seq = m // batch
    t = _tile(seq, t)
    assert t % CHUNK == 0 and vd == LANES
    nq = seq // t
    qmap = lambda b, qi, ki: (0, b * nq + qi, 0)
    kmap = lambda b, qi, ki: (0, b * nq + jnp.minimum(ki, qi), 0)
    return pl.pallas_call(
        _mla_flash_kernel,
        grid=(batch, nq, nq),
        in_specs=[pl.BlockSpec((heads, t, nope), qmap), pl.BlockSpec((heads, t, rd), qmap),
                  pl.BlockSpec((heads, t, nope), kmap),
                  pl.BlockSpec((t, rd), lambda b, qi, ki: (b * nq + jnp.minimum(ki, qi), 0)),
                  pl.BlockSpec((heads, t, vd), kmap)],
        out_specs=pl.BlockSpec((t, heads * vd), lambda b, qi, ki: (b * nq + qi, 0)),
        out_shape=jax.ShapeDtypeStruct((m, heads * vd), BF16),
        scratch_shapes=[pltpu.VMEM((heads, t, LANES), F32), pltpu.VMEM((heads, t, LANES), F32),
                        pltpu.VMEM((heads, t, vd), F32)],
        compiler_params=_cparams("parallel", "parallel", "arbitrary"),
        name="mla_flash",
    )(qn, qr, kn, kr, v)


def _mla_sample_kernel(qn_ref, qr_ref, ckv_ref, kr_ref, cckv_ref, ckr_ref, wuk_ref, wuv_ref, o_ref, ql_ref):
    nh, t, nope = qn_ref.shape
    kvl = ckv_ref.shape[1]
    vd = wuv_ref.shape[2]
    for hh in range(nh):
        ql_ref[hh] = _dot_nt(qn_ref[hh], wuk_ref[hh]).astype(BF16)
    q_lat = ql_ref[...].reshape(nh * t, kvl)
    q_rope = qr_ref[...].reshape(nh * t, qr_ref.shape[2])
    c_old, r_old = cckv_ref[0].astype(BF16), ckr_ref[0].astype(BF16)
    c_new, r_new = ckv_ref[...].astype(BF16), kr_ref[...].astype(BF16)
    s_old = _dot_nt(q_lat, c_old) + _dot_nt(q_rope, r_old)
    s_new = _dot_nt(q_lat, c_new) + _dot_nt(q_rope, r_new)
    mx = jnp.maximum(jnp.max(s_old, axis=-1, keepdims=True), jnp.max(s_new, axis=-1, keepdims=True))
    p_old, p_new = jnp.exp(s_old - mx), jnp.exp(s_new - mx)
    denom = jnp.sum(p_old, axis=-1, keepdims=True) + jnp.sum(p_new, axis=-1, keepdims=True)
    o_lat = (_dot(p_old.astype(BF16), c_old) + _dot(p_new.astype(BF16), c_new)) / denom
    o_lat = o_lat.astype(BF16)
    for hh in range(nh):
        o_ref[:, hh * vd:(hh + 1) * vd] = _dot(o_lat[hh * t:(hh + 1) * t], wuv_ref[hh]).astype(o_ref.dtype)


def _mla_sample(qn, qr, ckv, kr, cache_ckv, cache_kr, wuk_t, wuv_t):
    heads, m, nope = qn.shape
    batch, past, kvl = cache_ckv.shape
    t = m // batch
    rd = qr.shape[2]
    vd = wuv_t.shape[2]
    const3 = lambda b: (0, 0, 0)
    return pl.pallas_call(
        _mla_sample_kernel,
        grid=(batch,),
        in_specs=[pl.BlockSpec((heads, t, nope), lambda b: (0, b, 0)), pl.BlockSpec((heads, t, rd), lambda b: (0, b, 0)),
                  pl.BlockSpec((t, kvl), lambda b: (b, 0)), pl.BlockSpec((t, rd), lambda b: (b, 0)),
                  pl.BlockSpec((1, past, kvl), lambda b: (b, 0, 0)), pl.BlockSpec((1, past, rd), lambda b: (b, 0, 0)),
                  pl.BlockSpec(wuk_t.shape, const3), pl.BlockSpec(wuv_t.shape, const3)],
        out_specs=pl.BlockSpec((t, heads * vd), lambda b: (b, 0)),
        out_shape=jax.ShapeDtypeStruct((m, heads * vd), BF16),
        scratch_shapes=[pltpu.VMEM((heads, t, kvl), BF16)],
        compiler_params=_cparams("parallel"),
        name="mla_sample",
    )(qn, qr, ckv, kr, cache_ckv, cache_kr, wuk_t, wuv_t)


def _s5_kernel(*refs, has_h0, col_block):
    (x_ref, g_ref, wbr_ref, wbi_ref, ar_ref, ai_ref, cr_ref, ci_ref, d_ref), refs = refs[:9], refs[9:]
    if has_h0:
        (h0r_ref, h0i_ref), refs = refs[:2], refs[2:]
    y_ref, sr_ref, si_ref, br_ref, bi_ref, hr_ref, hi_ref = refs
    tt = pl.program_id(1)
    t, d = x_ref.shape
    n_state = br_ref.shape[1]
    n_gb, gin, gst = wbr_ref.shape

    @pl.when(tt == 0)
    def _():
        if has_h0:
            hr_ref[...] = h0r_ref[0]
            hi_ref[...] = h0i_ref[0]
        else:
            hr_ref[...] = jnp.zeros_like(hr_ref)
            hi_ref[...] = jnp.zeros_like(hi_ref)

    h = _rms(x_ref[...], g_ref[...])
    ub = h.astype(BF16)
    for gb in range(n_gb):
        u_blk = ub[:, gb * gin:(gb + 1) * gin]
        br_ref[:, gb * gst:(gb + 1) * gst] = _dot(u_blk, wbr_ref[gb])
        bi_ref[:, gb * gst:(gb + 1) * gst] = _dot(u_blk, wbi_ref[gb])

    for cb in range(n_state // col_block):
        sl = slice(cb * col_block, (cb + 1) * col_block)
        a_re, a_im = ar_ref[:, sl], ai_ref[:, sl]

        def step(s, carry):
            h_re, h_im = carry
            n_re = a_re * h_re - a_im * h_im + br_ref[pl.ds(s, 1), sl]
            n_im = a_re * h_im + a_im * h_re + bi_ref[pl.ds(s, 1), sl]
            br_ref[pl.ds(s, 1), sl] = n_re
            bi_ref[pl.ds(s, 1), sl] = n_im
            return n_re, n_im

        h_re, h_im = lax.fori_loop(0, t, step, (hr_ref[:, sl], hi_ref[:, sl]))
        hr_ref[:, sl] = h_re
        hi_ref[:, sl] = h_im

    for gb in range(n_gb):
        hs_re = br_ref[:, gb * gst:(gb + 1) * gst].astype(BF16)
        hs_im = bi_ref[:, gb * gst:(gb + 1) * gst].astype(BF16)
        cs = slice(gb * gin, (gb + 1) * gin)
        y = _dot(hs_re, cr_ref[gb]) - _dot(hs_im, ci_ref[gb]) + d_ref[:, cs] * h[:, cs]
        y_ref[:, cs] = _gelu_tanh(y).astype(y_ref.dtype)

    @pl.when(tt == pl.num_programs(1) - 1)
    def _():
        sr_ref[0] = hr_ref[...]
        si_ref[0] = hi_ref[...]


def _s5(x, g, wb_re, wb_im, a_re, a_im, wc_re, wc_im, d_skip, batch, h0=None, t=256, col_block=1024):
    m, d = x.shape
    seq = m // batch
    t = _tile(seq, t)
    nt = seq // t
    n_state = a_re.shape[1]
    col_block = _tile(n_state, col_block)
    const2 = lambda b, tt: (0, 0)
    const3 = lambda b, tt: (0, 0, 0)
    in_specs = [pl.BlockSpec((t, d), lambda b, tt: (b * nt + tt, 0)), pl.BlockSpec((1, d), const2),
                pl.BlockSpec(wb_re.shape, const3), pl.BlockSpec(wb_im.shape, const3),
                pl.BlockSpec((1, n_state), const2), pl.BlockSpec((1, n_state), const2),
                pl.BlockSpec(wc_re.shape, const3), pl.BlockSpec(wc_im.shape, const3),
                pl.BlockSpec((1, d), const2)]
    args = [x, g.reshape(1, d), wb_re, wb_im, a_re, a_im, wc_re, wc_im, d_skip.reshape(1, d)]
    if h0 is not None:
        in_specs += [pl.BlockSpec((1, 1, n_state), lambda b, tt: (b, 0, 0))] * 2
        args += [h0[0].reshape(batch, 1, n_state), h0[1].reshape(batch, 1, n_state)]
    state_spec = pl.BlockSpec((1, 1, n_state), lambda b, tt: (b, 0, 0))
    return pl.pallas_call(
        functools.partial(_s5_kernel, has_h0=h0 is not None, col_block=col_block),
        grid=(batch, nt),
        in_specs=in_specs,
        out_specs=[pl.BlockSpec((t, d), lambda b, tt: (b * nt + tt, 0)), state_spec, state_spec],
        out_shape=[jax.ShapeDtypeStruct((m, d), BF16), jax.ShapeDtypeStruct((batch, 1, n_state), F32),
                   jax.ShapeDtypeStruct((batch, 1, n_state), F32)],
        scratch_shapes=[pltpu.VMEM((t, n_state), F32), pltpu.VMEM((t, n_state), F32),
                        pltpu.VMEM((1, n_state), F32), pltpu.VMEM((1, n_state), F32)],
        compiler_params=_cparams("parallel", "arbitrary"),
        name="s5_scan",
    )(*args)


def _s5_params(a_re, a_im, log_dt, b_re, b_im, c_re, c_im, groups_per_block=8):
    g, p, gc = b_re.shape
    step = jnp.exp(log_dt)[:, None]
    xr, xi = a_re * step, a_im * step
    ab_re, ab_im = jnp.exp(xr) * jnp.cos(xi), jnp.exp(xr) * jnp.sin(xi)
    m1_re = jnp.expm1(xr) * jnp.cos(xi) - 2.0 * jnp.square(jnp.sin(0.5 * xi))
    den = a_re * a_re + a_im * a_im
    f_re = (m1_re * a_re + ab_im * a_im) / den
    f_im = (ab_im * a_re - m1_re * a_im) / den
    bb_re = f_re[..., None] * b_re - f_im[..., None] * b_im
    bb_im = f_re[..., None] * b_im + f_im[..., None] * b_re
    gpb = groups_per_block
    eye = jnp.eye(gpb, dtype=F32)

    def blockdiag_b(w):
        w = w.reshape(g // gpb, gpb, p, gc)
        return jnp.einsum('bgpc,gh->bgchp', w, eye).reshape(g // gpb, gpb * gc, gpb * p).astype(BF16)

    def blockdiag_c(w):
        w = w.reshape(g // gpb, gpb, gc, p)
        return jnp.einsum('bgcp,gh->bgphc', w, eye).reshape(g // gpb, gpb * p, gpb * gc).astype(BF16)

    return (blockdiag_b(bb_re), blockdiag_b(bb_im), ab_re.reshape(1, g * p), ab_im.reshape(1, g * p),
            blockdiag_c(c_re), blockdiag_c(c_im))


def _sb_block(q, k, v, mask, tri, r_prev, scale):
    z = _dot_nt(q, k) * scale
    sp = _softplus(z)
    if mask is not None:
        sp = jnp.where(mask, sp, 0.0)
    hi = sp.astype(BF16)
    lo = (sp - hi.astype(F32)).astype(BF16)
    csum = _dot(hi, tri) + _dot(lo, tri)
    log_w = z - csum
    if mask is not None:
        log_w = jnp.where(mask, log_w, -jnp.inf)
    n_keys = z.shape[1]
    if n_keys % LANES:
        w = jnp.exp(log_w - r_prev[:, :n_keys])
    else:
        w = jnp.concatenate([jnp.exp(log_w[:, c * LANES:(c + 1) * LANES] - r_prev) for c in range(n_keys // LANES)],
                            axis=-1)
    return _dot(w.astype(BF16), v), r_prev + jnp.sum(sp, axis=-1, keepdims=True)


def _tri(n):
    row = lax.broadcasted_iota(jnp.int32, (n, n), 0)
    col = lax.broadcasted_iota(jnp.int32, (n, n), 1)
    return (row >= col).astype(BF16)


def _sb_flash_kernel(q_ref, k_hbm, v_hbm, o_ref, kbuf, vbuf, sem, r_ref, acc_ref, *, scale, seq):
    b, qi = pl.program_id(0), pl.program_id(1)
    nh, t, hd = acc_ref.shape
    row0 = b * seq + qi * t

    def copies(kj, slot):
        start = pl.multiple_of(row0 - kj * t, t)
        return (pltpu.make_async_copy(k_hbm.at[:, pl.ds(start, t), :], kbuf.at[slot], sem.at[0, slot]),
                pltpu.make_async_copy(v_hbm.at[:, pl.ds(start, t), :], vbuf.at[slot], sem.at[1, slot]))

    for cp in copies(0, 0):
        cp.start()
    r_ref[...] = jnp.zeros_like(r_ref)
    acc_ref[...] = jnp.zeros_like(acc_ref)
    row = lax.broadcasted_iota(jnp.int32, (t, t), 0)
    col = lax.broadcasted_iota(jnp.int32, (t, t), 1)
    tri = _tri(t)

    def sweep(carry):
        kj, _ = carry
        slot = kj % 2
        for cp in copies(kj, slot):
            cp.wait()

        @pl.when(kj < qi)
        def _():
            for cp in copies(kj + 1, 1 - slot):
                cp.start()

        mask = (col < row) | (kj > 0)

        def head(hh, r_min):
            o, r = _sb_block(q_ref[hh], kbuf[slot, hh], vbuf[slot, hh], mask, tri, r_ref[hh], scale)
            acc_ref[hh] += o
            r_ref[hh] = r
            return jnp.minimum(r_min, jnp.min(r))

        r_min = lax.fori_loop(0, nh, head, jnp.float32(jnp.inf), unroll=2)
        return kj + 1, (r_min > SB_EXIT_SUM).astype(jnp.int32)

    kj_end, _ = lax.while_loop(lambda c: (c[0] <= qi) & (c[1] == 0), sweep, (jnp.int32(0), jnp.int32(0)))

    @pl.when(kj_end <= qi)
    def _():
        for cp in copies(kj_end, kj_end % 2):
            cp.wait()

    for hh in range(nh):
        o_ref[:, hh * hd:(hh + 1) * hd] = acc_ref[hh].astype(o_ref.dtype)


def _sb_flash(q, k, v, batch, scale, t=256):
    heads, m, hd = q.shape
    seq = m // batch
    t = _tile(seq, t)
    nq = seq // t
    return pl.pallas_call(
        functools.partial(_sb_flash_kernel, scale=scale, seq=seq),
        grid=(batch, nq),
        in_specs=[pl.BlockSpec((heads, t, hd), lambda b, qi: (0, b * nq + qi, 0)),
                  pl.BlockSpec(memory_space=pl.ANY), pl.BlockSpec(memory_space=pl.ANY)],
        out_specs=pl.BlockSpec((t, heads * hd), lambda b, qi: (b * nq + qi, 0)),
        out_shape=jax.ShapeDtypeStruct((m, heads * hd), BF16),
        scratch_shapes=[pltpu.VMEM((2, heads, t, hd), BF16), pltpu.VMEM((2, heads, t, hd), BF16),
                        pltpu.SemaphoreType.DMA((2, 2)),
                        pltpu.VMEM((heads, t, LANES), F32), pltpu.VMEM((heads, t, hd), F32)],
        compiler_params=_cparams("parallel", "parallel"),
        name="sb_flash",
    )(q, k, v)


def _sb_sample_kernel(q_ref, kn_ref, vn_ref, kc_hbm, vc_hbm, o_ref, kbuf, vbuf, sem, r_ref, acc_ref, *,
                      scale, layer, blk):
    b = pl.program_id(0)
    nh, t, hd = acc_ref.shape
    nblk = kc_hbm.shape[2] // blk

    def copies(kb, slot):
        start = pl.multiple_of((nblk - 1 - kb) * blk, blk)
        cps = []
        for hh in range(nh):
            cps.append(pltpu.make_async_copy(kc_hbm.at[layer, b, pl.ds(start, blk), hh], kbuf.at[slot, hh],
                                             sem.at[0, slot]))
            cps.append(pltpu.make_async_copy(vc_hbm.at[layer, b, pl.ds(start, blk), hh], vbuf.at[slot, hh],
                                             sem.at[1, slot]))
        return cps

    for cp in copies(0, 0):
        cp.start()
    row = lax.broadcasted_iota(jnp.int32, (t, t), 0)
    col = lax.broadcasted_iota(jnp.int32, (t, t), 1)
    own_mask, own_tri = col < row, _tri(t)

    def own(hh, carry):
        o, r = _sb_block(q_ref[hh], kn_ref[hh], vn_ref[hh], own_mask, own_tri, jnp.zeros((t, LANES), F32), scale)
        acc_ref[hh] = o
        r_ref[hh] = r
        return carry

    lax.fori_loop(0, nh, own, 0)
    tri = _tri(blk)

    def sweep(carry):
        kb, _ = carry
        slot = kb % 2
        for cp in copies(kb, slot):
            cp.wait()

        @pl.when(kb + 1 < nblk)
        def _():
            for cp in copies(kb + 1, 1 - slot):
                cp.start()

        def head(hh, r_min):
            o, r = _sb_block(q_ref[hh], kbuf[slot, hh].astype(BF16), vbuf[slot, hh].astype(BF16), None, tri,
                             r_ref[hh], scale)
            acc_ref[hh] += o
            r_ref[hh] = r
            return jnp.minimum(r_min, jnp.min(r))

        r_min = lax.fori_loop(0, nh, head, jnp.float32(jnp.inf))
        return kb + 1, (r_min > SB_EXIT_SUM).astype(jnp.int32)

    kb_end, _ = lax.while_loop(lambda c: (c[0] < nblk) & (c[1] == 0), sweep, (jnp.int32(0), jnp.int32(0)))

    @pl.when(kb_end < nblk)
    def _():
        for cp in copies(kb_end, kb_end % 2):
            cp.wait()

    for hh in range(nh):
        o_ref[:, hh * hd:(hh + 1) * hd] = acc_ref[hh].astype(o_ref.dtype)


def _sb_sample(q, k_new, v_new, cache_k, cache_v, layer, scale, blk=256):
    heads, m, hd = q.shape
    batch, past = cache_k.shape[1], cache_k.shape[2]
    t = m // batch
    blk = _tile(past, blk)
    new = lambda b: (0, b, 0)
    return pl.pallas_call(
        functools.partial(_sb_sample_kernel, scale=scale, layer=layer, blk=blk),
        grid=(batch,),
        in_specs=[pl.BlockSpec((heads, t, hd), new), pl.BlockSpec((heads, t, hd), new),
                  pl.BlockSpec((heads, t, hd), new),
                  pl.BlockSpec(memory_space=pl.ANY), pl.BlockSpec(memory_space=pl.ANY)],
        out_specs=pl.BlockSpec((t, heads * hd), lambda b: (b, 0)),
        out_shape=jax.ShapeDtypeStruct((m, heads * hd), BF16),
        scratch_shapes=[pltpu.VMEM((2, heads, blk, hd), F32), pltpu.VMEM((2, heads, blk, hd), F32),
                        pltpu.SemaphoreType.DMA((2, 2)),
                        pltpu.VMEM((heads, t, LANES), F32), pltpu.VMEM((heads, t, hd), F32)],
        compiler_params=_cparams("parallel"),
        name="sb_sample",
    )(q, k_new, v_new, cache_k, cache_v)


def _rope_tables(pos, d, reps):
    half = d // 2
    inv = ROPE_BASE ** (-jnp.arange(half, dtype=F32) * 2.0 / d)
    ang = pos.astype(F32)[:, None] * inv[None, :]
    return jnp.tile(jnp.cos(ang), (1, reps)), jnp.tile(jnp.sin(ang), (1, reps))


def _rot_half_cols(w, d):
    k, n = w.shape
    w = w.reshape(k, n // d, d)
    half = d // 2
    return jnp.concatenate([-w[..., half:], w[..., :half]], axis=-1).reshape(k, n)


def kernel(x_prompt, x_sample, state_ret, cache_mla_ckv, cache_mla_krope, state_s5_re, state_s5_im, cache_sb_k, cache_sb_v, state_ffn_conv, norm_gains, ret_w_in, ret_w_out, mla_w_down, mla_g_q, mla_g_kv, mla_w_uq, mla_w_uk, mla_w_uv, mla_w_o, s5_a_re, s5_a_im, s5_log_dt, s5_b_re, s5_b_im, s5_c_re, s5_c_im, s5_d, s5_w_glu, s5_b_glu, sb_w_qkv, sb_w_o, ffn_w_up, ffn_conv_w, ffn_conv_b, ffn_w_down):
    n_p, seq_p, d = x_prompt.shape
    n_s, seq_s, _ = x_sample.shape
    depth = norm_gains.shape[0]
    n_mixers = 4
    streams = (dict(batch=n_p, seq=seq_p, pos=jnp.arange(seq_p)),
               dict(batch=n_s, seq=seq_s, pos=None))
    xs = [x_prompt.reshape(n_p * seq_p, d), x_sample.reshape(n_s * seq_s, d)]
    hs = [_rmsnorm(x, norm_gains[0, 0]) for x in xs]
    outs = {name: ([], []) for name in ('ret', 'ckv', 'kr', 's5re', 's5im', 'sbk', 'sbv', 'conv')}

    past_len = cache_mla_ckv.shape[2]

    def positions(si):
        if si == 0:
            return jnp.arange(seq_p)
        return jnp.tile(past_len + jnp.arange(seq_s), n_s)

    for i in range(depth):
        j, kind = i // n_mixers, i % n_mixers
        g = norm_gains[i]
        g_next = norm_gains[i + 1, 0] if i + 1 < depth else g[0]
        emit_next = i + 1 < depth and (i + 1) % n_mixers != 2
        for si, st in enumerate(streams):
            x, h, batch = xs[si], hs[si], st['batch']
            if kind == 0:
                heads, dk, dv = state_ret.shape[2], state_ret.shape[3], state_ret.shape[4]
                qk_w, v_w = heads * dk, heads * dv
                w_in = ret_w_in[j].astype(BF16)
                cos, sin = _rope_tables(positions(si), dk, 1)
                qk = _linear_rope(h, w_in, 2 * qk_w, cos, sin, dk, qk_w, dk ** -0.5)
                v, = _linear(h, w_in, 2 * qk_w, v_w, (BF16,), name="ret_v")
                gate, = _linear(h, w_in, 2 * qk_w + v_w, v_w, (F32,), name="ret_g")
                s0 = state_ret[j] if si == 1 else None
                y, s1 = _retention(qk, v, gate, batch, heads, s0)
                outs['ret'][si].append(s1)
                x, h = _outproj(y, ret_w_out[j].astype(BF16), x, g[1], g[2], name="ret_outproj")
            elif kind == 1:
                ql, kvl = mla_g_q.shape[1], mla_g_kv.shape[1]
                heads, nope = mla_w_uk.shape[2], mla_w_uk.shape[3]
                rd, vd = cache_mla_krope.shape[3], mla_w_uv.shape[3]
                past = cache_mla_ckv.shape[2]
                scale = (nope + rd) ** -0.5
                w_down = mla_w_down[j]
                w_kr = w_down[:, ql + kvl:]
                pos = positions(si)
                cos_k, sin_k = _rope_tables(pos, rd, 2)
                cos_q, sin_q = _rope_tables(pos, rd, 2 * LANES // rd)
                cq, ckv, kr = _mla_down(h, w_down[:, :ql].astype(BF16), w_down[:, ql:ql + kvl].astype(BF16),
                                        w_kr.astype(BF16), _rot_half_cols(w_kr, rd).astype(BF16),
                                        mla_g_q[j], mla_g_kv[j], cos_k, sin_k)
                w_uq = mla_w_uq[j].reshape(ql, heads, nope + rd)
                w_qn = w_uq[:, :, :nope].reshape(ql, heads * nope)
                w_qr = w_uq[:, :, nope:].reshape(ql, heads * rd)
                qn, qr = _mla_q(cq, w_qn.astype(BF16), w_qr.astype(BF16), _rot_half_cols(w_qr, rd).astype(BF16),
                                cos_q, sin_q, heads, scale)
                outs['ckv'][si].append(ckv.reshape(batch, st['seq'], kvl))
                outs['kr'][si].append(kr.reshape(batch, st['seq'], rd))
                if si == 0:
                    kn, = _linear(ckv, mla_w_uk[j].reshape(kvl, heads * nope).astype(BF16), 0, heads * nope,
                                  (BF16,), (True,), name="mla_k_nope")
                    v, = _linear(ckv, mla_w_uv[j].reshape(kvl, heads * vd).astype(BF16), 0, heads * vd,
                                 (BF16,), (True,), name="mla_v")
                    o = _mla_flash(qn, qr, kn, kr, v, batch)
                else:
                    o = _mla_sample(qn, qr, ckv, kr, cache_mla_ckv[j], cache_mla_krope[j],
                                    jnp.transpose(mla_w_uk[j], (1, 0, 2)).astype(BF16),
                                    jnp.transpose(mla_w_uv[j], (1, 0, 2)).astype(BF16))
                x, h = _outproj(o, mla_w_o[j].astype(BF16), x, g[1], g[2], name="mla_outproj")
            elif kind == 2:
                params = _s5_params(s5_a_re[j], s5_a_im[j], s5_log_dt[j], s5_b_re[j], s5_b_im[j],
                                    s5_c_re[j], s5_c_im[j])
                h0 = (state_s5_re[j], state_s5_im[j]) if si == 1 else None
                y, s_re, s_im = _s5(x, g[0], *params, s5_d[j], batch, h0)
                outs['s5re'][si].append(s_re.reshape(batch, *s5_a_re.shape[1:]))
                outs['s5im'][si].append(s_im.reshape(batch, *s5_a_re.shape[1:]))
                x, h = _glu_outproj(y, s5_w_glu[j].astype(BF16), s5_b_glu[j], x, g[1], g[2])
            else:
                heads, hd = cache_sb_k.shape[3], cache_sb_k.shape[4]
                past = cache_sb_k.shape[2]
                w_qkv = sb_w_qkv[j].astype(BF16)
                scale = hd ** -0.5
                q, = _linear(h, w_qkv, 0, heads * hd, (BF16,), (True,), name="sb_q")
                k32, k16 = _linear(h, w_qkv, heads * hd, heads * hd, (F32, BF16), (False, True), name="sb_k")
                v32, v16 = _linear(h, w_qkv, 2 * heads * hd, heads * hd, (F32, BF16), (False, True), name="sb_v")
                outs['sbk'][si].append(k32.reshape(batch, st['seq'], heads, hd))
                outs['sbv'][si].append(v32.reshape(batch, st['seq'], heads, hd))
                if si == 0:
                    o = _sb_flash(q, k16, v16, batch, scale)
                else:
                    o = _sb_sample(q, k16, v16, cache_sb_k, cache_sb_v, j, scale)
                x, h = _outproj(o, sb_w_o[j].astype(BF16), x, g[1], g[2], name="sb_outproj")
            x, h, conv = _conv_ffn(h, x, ffn_w_up[i].astype(BF16), ffn_conv_w[i], ffn_conv_b[i],
                                   ffn_w_down[i].astype(BF16), g[3], g_next, st['seq'],
                                   state=state_ffn_conv[i] if si == 1 else None, emit_h=emit_next)
            outs['conv'][si].append(conv)
            xs[si], hs[si] = x, h

    stack = lambda name: (jnp.stack(outs[name][0]), jnp.stack(outs[name][1]))
    return (xs[0].reshape(x_prompt.shape), xs[1].reshape(x_sample.shape),
            *stack('ret'), *stack('ckv'), *stack('kr'), *stack('s5re'), *stack('s5im'),
            *stack('sbk'), *stack('sbv'), *stack('conv'))
```

```python
import functools
import math

import jax
import jax.numpy as jnp
from jax import lax
from jax.experimental import pallas as pl
from jax.experimental.pallas import tpu as pltpu

F32 = jnp.float32
BF16 = jnp.bfloat16
NORM_EPS = 1e-6
GN_EPS = 1e-5
ROPE_BASE = 10000.0
CHUNK = 64
CONV_W = 3
LANES = 128
VMEM_LIMIT = 56 * 1024 * 1024
SB_EXIT_SUM = 104.0
FFN_COLS = 512


def _cparams(*sem):
    return pltpu.CompilerParams(dimension_semantics=sem, vmem_limit_bytes=VMEM_LIMIT)


def _tile(n, pref):
    t = min(n, pref)
    while n % t:
        t -= 1
    return t


def _dot(a, b):
    return jnp.dot(a, b, preferred_element_type=F32)


def _dot_nt(a, b):
    return lax.dot_general(a, b, (((1,), (1,)), ((), ())), preferred_element_type=F32)


def _rms(x, g):
    return x * lax.rsqrt(jnp.mean(x * x, axis=-1, keepdims=True) + NORM_EPS) * g


def _gelu_tanh(x):
    return 0.5 * x * (1.0 + jnp.tanh(math.sqrt(2.0 / math.pi) * (x + 0.044715 * (x * x * x))))


def _softplus(z):
    return jnp.maximum(z, 0.0) + jnp.log(1.0 + jnp.exp(-jnp.abs(z)))


def _rmsnorm_kernel(x_ref, g_ref, o_ref):
    o_ref[...] = _rms(x_ref[...], g_ref[...]).astype(o_ref.dtype)


def _rmsnorm(x, g, out_dtype=BF16, tm=512):
    m, d = x.shape
    tm = _tile(m, tm)
    return pl.pallas_call(
        _rmsnorm_kernel,
        grid=(m // tm,),
        in_specs=[pl.BlockSpec((tm, d), lambda i: (i, 0)), pl.BlockSpec((1, d), lambda i: (0, 0))],
        out_specs=pl.BlockSpec((tm, d), lambda i: (i, 0)),
        out_shape=jax.ShapeDtypeStruct((m, d), out_dtype),
        compiler_params=_cparams("parallel"),
        name="rmsnorm",
    )(x, g.reshape(1, d))


def _linear_kernel(a_ref, w_ref, *o_refs, head_major):
    acc = _dot(a_ref[...].astype(BF16), w_ref[...])
    for o_ref in o_refs:
        if head_major and len(o_ref.shape) == 3:
            for hh in range(o_ref.shape[0]):
                o_ref[hh] = acc[:, hh * LANES:(hh + 1) * LANES].astype(o_ref.dtype)
        else:
            o_ref[...] = acc.astype(o_ref.dtype)


def _linear(a, w, col_off, n_out, out_dtypes, head_major=(), tm=1024, tn=512, name="linear"):
    m, k = a.shape
    tm, tn = _tile(m, tm), _tile(n_out, tn)
    assert col_off % tn == 0 and tn % LANES == 0
    joff = col_off // tn
    out_specs, out_shape = [], []
    for dt, hm in zip(out_dtypes, head_major or (False,) * len(out_dtypes)):
        if hm:
            out_specs.append(pl.BlockSpec((tn // LANES, tm, LANES), lambda i, j: (j, i, 0)))
            out_shape.append(jax.ShapeDtypeStruct((n_out // LANES, m, LANES), dt))
        else:
            out_specs.append(pl.BlockSpec((tm, tn), lambda i, j: (i, j)))
            out_shape.append(jax.ShapeDtypeStruct((m, n_out), dt))
    return pl.pallas_call(
        functools.partial(_linear_kernel, head_major=bool(head_major)),
        grid=(m // tm, n_out // tn),
        in_specs=[pl.BlockSpec((tm, k), lambda i, j: (i, 0)),
                  pl.BlockSpec((k, tn), lambda i, j: (0, joff + j))],
        out_specs=out_specs,
        out_shape=out_shape,
        compiler_params=_cparams("parallel", "arbitrary"),
        name=name,
    )(a, w)


def _linear_rope_kernel(a_ref, w_ref, cos_ref, sin_ref, o_ref, *, dk, n_q_tiles, k_scale):
    acc = _dot(a_ref[...], w_ref[...])
    scale = jnp.where(pl.program_id(1) >= n_q_tiles, k_scale, 1.0).astype(F32)
    cos, sin = cos_ref[...] * scale, sin_ref[...] * scale
    half = dk // 2
    for hh in range(acc.shape[1] // dk):
        x1 = acc[:, hh * dk:hh * dk + half]
        x2 = acc[:, hh * dk + half:(hh + 1) * dk]
        o_ref[:, hh * dk:hh * dk + half] = x1 * cos - x2 * sin
        o_ref[:, hh * dk + half:(hh + 1) * dk] = x1 * sin + x2 * cos


def _linear_rope(a, w, n_out, cos, sin, dk, n_q, k_scale, tm=1024, tn=512):
    m, k = a.shape
    ltab = cos.shape[0]
    tm = _tile(math.gcd(m, ltab), tm)
    tn = _tile(n_out, tn)
    ntab = ltab // tm
    return pl.pallas_call(
        functools.partial(_linear_rope_kernel, dk=dk, n_q_tiles=n_q // tn, k_scale=k_scale),
        grid=(m // tm, n_out // tn),
        in_specs=[pl.BlockSpec((tm, k), lambda i, j: (i, 0)),
                  pl.BlockSpec((k, tn), lambda i, j: (0, j)),
                  pl.BlockSpec((tm, dk // 2), lambda i, j: (i % ntab, 0)),
                  pl.BlockSpec((tm, dk // 2), lambda i, j: (i % ntab, 0))],
        out_specs=pl.BlockSpec((tm, tn), lambda i, j: (i, j)),
        out_shape=jax.ShapeDtypeStruct((m, n_out), F32),
        compiler_params=_cparams("parallel", "arbitrary"),
        name="ret_qk_rope",
    )(a, w, cos, sin)


def _outproj_kernel(a_ref, w_ref, x_ref, gp_ref, gn_ref, xo_ref, *rest, emit_h):
    acc_ref = rest[-1]
    kk = pl.program_id(1)

    @pl.when(kk == 0)
    def _():
        acc_ref[...] = jnp.zeros_like(acc_ref)

    acc_ref[...] += _dot(a_ref[...].astype(BF16), w_ref[...])

    @pl.when(kk == pl.num_programs(1) - 1)
    def _():
        xn = x_ref[...] + _rms(acc_ref[...], gp_ref[...])
        xo_ref[...] = xn
        if emit_h:
            rest[0][...] = _rms(xn, gn_ref[...]).astype(BF16)


def _outproj(a, w, x, g_post, g_next, emit_h=True, tm=512, tk=512, name="outproj"):
    m, k = a.shape
    d = w.shape[1]
    tm, tk = _tile(m, tm), _tile(k, tk)
    out_specs = [pl.BlockSpec((tm, d), lambda i, kk: (i, 0))]
    out_shape = [jax.ShapeDtypeStruct((m, d), F32)]
    if emit_h:
        out_specs.append(pl.BlockSpec((tm, d), lambda i, kk: (i, 0)))
        out_shape.append(jax.ShapeDtypeStruct((m, d), BF16))
    res = pl.pallas_call(
        functools.partial(_outproj_kernel, emit_h=emit_h),
        grid=(m // tm, k // tk),
        in_specs=[pl.BlockSpec((tm, tk), lambda i, kk: (i, kk)),
                  pl.BlockSpec((tk, d), lambda i, kk: (kk, 0)),
                  pl.BlockSpec((tm, d), lambda i, kk: (i, 0)),
                  pl.BlockSpec((1, d), lambda i, kk: (0, 0)),
                  pl.BlockSpec((1, d), lambda i, kk: (0, 0))],
        out_specs=out_specs,
        out_shape=out_shape,
        scratch_shapes=[pltpu.VMEM((tm, d), F32)],
        compiler_params=_cparams("parallel", "arbitrary"),
        name=name,
    )(a, w, x, g_post.reshape(1, d), g_next.reshape(1, d))
    return (res[0], res[1]) if emit_h else (res[0], None)


def _glu_outproj_kernel(a_ref, wv_ref, wg_ref, bv_ref, bg_ref, x_ref, gp_ref, gn_ref, xo_ref, ho_ref,
                        accv_ref, accg_ref):
    kk = pl.program_id(1)

    @pl.when(kk == 0)
    def _():
        accv_ref[...] = jnp.zeros_like(accv_ref)
        accg_ref[...] = jnp.zeros_like(accg_ref)

    a = a_ref[...]
    accv_ref[...] += _dot(a, wv_ref[...])
    accg_ref[...] += _dot(a, wg_ref[...])

    @pl.when(kk == pl.num_programs(1) - 1)
    def _():
        mix = (accv_ref[...] + bv_ref[...]) * jax.nn.sigmoid(accg_ref[...] + bg_ref[...])
        xn = x_ref[...] + _rms(mix, gp_ref[...])
        xo_ref[...] = xn
        ho_ref[...] = _rms(xn, gn_ref[...]).astype(BF16)


def _glu_outproj(a, w, b, x, g_post, g_next, tm=512, tk=512):
    m, k = a.shape
    d = w.shape[1] // 2
    tm, tk = _tile(m, tm), _tile(k, tk)
    row = lambda i, kk: (i, 0)
    const = lambda i, kk: (0, 0)
    return pl.pallas_call(
        _glu_outproj_kernel,
        grid=(m // tm, k // tk),
        in_specs=[pl.BlockSpec((tm, tk), lambda i, kk: (i, kk)),
                  pl.BlockSpec((tk, d), lambda i, kk: (kk, 0)),
                  pl.BlockSpec((tk, d), lambda i, kk: (kk, 1)),
                  pl.BlockSpec((1, d), lambda i, kk: (0, 0)),
                  pl.BlockSpec((1, d), lambda i, kk: (0, 1)),
                  pl.BlockSpec((tm, d), row), pl.BlockSpec((1, d), const), pl.BlockSpec((1, d), const)],
        out_specs=[pl.BlockSpec((tm, d), row), pl.BlockSpec((tm, d), row)],
        out_shape=[jax.ShapeDtypeStruct((m, d), F32), jax.ShapeDtypeStruct((m, d), BF16)],
        scratch_shapes=[pltpu.VMEM((tm, d), F32), pltpu.VMEM((tm, d), F32)],
        compiler_params=_cparams("parallel", "arbitrary"),
        name="s5_glu_outproj",
    )(a, w, w, b.reshape(1, 2 * d), b.reshape(1, 2 * d), x, g_post.reshape(1, d), g_next.reshape(1, d))


def _ffn_kernel(*refs, n_seq, has_state, emit_h, tiles_per_seq, nf):
    (h_ref, x_ref, wv_ref, wg_ref, cwv_ref, cwg_ref, cbv_ref, cbg_ref, wd_ref, gp_ref, gn_ref), refs = refs[:11], refs[11:]
    if has_state:
        (stv_ref, stg_ref), refs = refs[:2], refs[2:]
    xo_ref, refs = refs[0], refs[1:]
    if emit_h:
        ho_ref, refs = refs[0], refs[1:]
    cv_ref, cg_ref, acc_ref, act_even_ref, act_odd_ref = refs[:5]
    carry_refs = refs[5:]
    i, j = pl.program_id(0), pl.program_id(1)
    tm, tf = h_ref.shape[0], wv_ref.shape[2]
    seq = tm // n_seq

    if not has_state:
        @pl.when((i == 0) & (j == 0))
        def _():
            for carry_ref in carry_refs[:2]:
                carry_ref[...] = jnp.zeros_like(carry_ref)

    def conv_with_state(w_ref, cw_ref, cb_ref, st_ref, c_out_ref):
        up = _dot(h_ref[...], w_ref[0]).reshape(n_seq, seq, tf)
        prev1, prev2 = st_ref[:, 1:2, :], st_ref[:, 0:1, :]
        pos = lax.broadcasted_iota(jnp.int32, up.shape, 1)
        p1 = jnp.where(pos == 0, prev1, pltpu.roll(up, 1, axis=1))
        p2 = jnp.where(pos == 0, prev2, jnp.where(pos == 1, prev1, pltpu.roll(up, 2, axis=1)))
        c_out_ref[...] = up[:, seq - 2:, :]
        cw = cw_ref[...]
        c = cb_ref[...][None] + cw[0:1][None] * p2 + cw[1:2][None] * p1 + cw[2:3][None] * up
        return c.reshape(tm, tf)

    def conv_with_carry(w_ref, cw_ref, cb_ref, carry_ref, ext_ref, c_out_ref):
        up = _dot(h_ref[...], w_ref[0])
        ext_ref[0:8, :] = jnp.where(i % tiles_per_seq != 0, carry_ref[j], 0.0)
        ext_ref[8:, :] = up
        carry_ref[j] = up[tm - 8:, :]
        c_out_ref[0] = up[tm - 2:, :]
        cw = cw_ref[...]
        return cb_ref[...] + cw[0:1] * ext_ref[6:6 + tm, :] + cw[1:2] * ext_ref[7:7 + tm, :] + cw[2:3] * up

    def activation():
        if has_state:
            val = conv_with_state(wv_ref, cwv_ref, cbv_ref, stv_ref, cv_ref)
            gate = conv_with_state(wg_ref, cwg_ref, cbg_ref, stg_ref, cg_ref)
        else:
            val = conv_with_carry(wv_ref, cwv_ref, cbv_ref, carry_refs[0], carry_refs[2], cv_ref)
            gate = conv_with_carry(wg_ref, cwg_ref, cbg_ref, carry_refs[1], carry_refs[3], cg_ref)
        return (_gelu_tanh(gate) * val).astype(BF16)

    @pl.when(j == 0)
    def _():
        acc_ref[...] = jnp.zeros_like(acc_ref)
        act_odd_ref[...] = jnp.zeros_like(act_odd_ref)

    def step(act_new_ref, act_old_ref):
        act_new_ref[...] = activation()
        acc_ref[...] += _dot(act_old_ref[...], wd_ref[...])

    @pl.when((j < nf) & (j % 2 == 0))
    def _():
        step(act_even_ref, act_odd_ref)

    @pl.when((j < nf) & (j % 2 == 1))
    def _():
        step(act_odd_ref, act_even_ref)

    @pl.when(j == nf)
    def _():
        last = act_even_ref if (nf - 1) % 2 == 0 else act_odd_ref
        acc_ref[...] += _dot(last[...], wd_ref[...])
        xn = x_ref[...] + _rms(acc_ref[...], gp_ref[...])
        xo_ref[...] = xn
        if emit_h:
            ho_ref[...] = _rms(xn, gn_ref[...]).astype(BF16)


def _block_cols(w, tn):
    k, n = w.shape
    return jnp.transpose(w.reshape(k, n // tn, tn), (1, 0, 2)).astype(BF16)


def _conv_ffn(h, x, w_up, conv_w, conv_b, w_down, g_post, g_next, seq_len, state=None, emit_h=True, tm=512):
    m, d = h.shape
    d_ff = w_down.shape[0]
    tf = w_up.shape[2]
    nf = d_ff // tf
    has_state = state is not None
    if has_state:
        tm = _tile(m // seq_len, max(tm // seq_len, 1)) * seq_len
        n_seq, tiles_per_seq = tm // seq_len, 1
    else:
        tm = _tile(seq_len, tm)
        n_seq, tiles_per_seq = 1, seq_len // tm
    n_seqs = m // seq_len
    row = lambda i, j: (i, 0)
    const = lambda i, j: (0, 0)
    up = lambda j: jnp.minimum(j, nf - 1)
    val_cols = lambda i, j: (0, up(j))
    gate_cols = lambda i, j: (0, nf + up(j))
    in_specs = [pl.BlockSpec((tm, d), row), pl.BlockSpec((tm, d), row),
                pl.BlockSpec((1, d, tf), lambda i, j: (up(j), 0, 0)),
                pl.BlockSpec((1, d, tf), lambda i, j: (nf + up(j), 0, 0)),
                pl.BlockSpec((CONV_W, tf), val_cols), pl.BlockSpec((CONV_W, tf), gate_cols),
                pl.BlockSpec((1, tf), val_cols), pl.BlockSpec((1, tf), gate_cols),
                pl.BlockSpec((tf, d), lambda i, j: (jnp.maximum(j - 1, 0), 0)),
                pl.BlockSpec((1, d), const), pl.BlockSpec((1, d), const)]
    args = [h, x, w_up, w_up, conv_w, conv_w, conv_b.reshape(1, -1), conv_b.reshape(1, -1), w_down,
            g_post.reshape(1, d), g_next.reshape(1, d)]
    if has_state:
        in_specs += [pl.BlockSpec((n_seq, CONV_W - 1, tf), lambda i, j: (i, 0, up(j))),
                     pl.BlockSpec((n_seq, CONV_W - 1, tf), lambda i, j: (i, 0, nf + up(j)))]
        args += [state, state]
        scratch = []
    else:
        scratch = [pltpu.VMEM((nf, 8, tf), F32), pltpu.VMEM((nf, 8, tf), F32),
                   pltpu.VMEM((tm + 8, tf), F32), pltpu.VMEM((tm + 8, tf), F32)]
    conv_map = lambda i, j: (i, 0, up(j))
    out_specs = [pl.BlockSpec((tm, d), row)]
    out_shape = [jax.ShapeDtypeStruct((m, d), F32)]
    if emit_h:
        out_specs.append(pl.BlockSpec((tm, d), row))
        out_shape.append(jax.ShapeDtypeStruct((m, d), BF16))
    out_specs += [pl.BlockSpec((n_seq, CONV_W - 1, tf), conv_map)] * 2
    out_shape += [jax.ShapeDtypeStruct((n_seqs * tiles_per_seq, CONV_W - 1, d_ff), F32)] * 2
    res = pl.pallas_call(
        functools.partial(_ffn_kernel, n_seq=n_seq, has_state=has_state, emit_h=emit_h, tiles_per_seq=tiles_per_seq,
                          nf=nf),
        grid=(m // tm, nf + 1),
        in_specs=in_specs,
        out_specs=out_specs,
        out_shape=out_shape,
        scratch_shapes=[pltpu.VMEM((tm, d), F32), pltpu.VMEM((tm, tf), BF16), pltpu.VMEM((tm, tf), BF16)] + scratch,
        compiler_params=_cparams("arbitrary", "arbitrary"),
        name="conv_ffn_state" if has_state else "conv_ffn",
    )(*args)
    x_new = res[0]
    h_new = res[1] if emit_h else None
    conv_state = jnp.concatenate(res[-2:], axis=-1)[tiles_per_seq - 1::tiles_per_seq]
    return x_new, h_new, conv_state


def _ret_kernel(*refs, has_s0):
    if has_s0:
        lg_ref, q_ref, k_ref, v_ref, g_ref, s0_ref, y_ref, so_ref, s_ref = refs
    else:
        lg_ref, q_ref, k_ref, v_ref, g_ref, y_ref, so_ref, s_ref = refs
    c = pl.program_id(2)
    cl = q_ref.shape[0]

    @pl.when(c == 0)
    def _():
        s_ref[...] = s0_ref[0, 0] if has_s0 else jnp.zeros_like(s_ref)

    lg = lg_ref[0, 0:1, 0:1]
    q, k, v = q_ref[...], k_ref[...], v_ref[...]
    row = lax.broadcasted_iota(jnp.int32, (cl, cl), 0)
    col = lax.broadcasted_iota(jnp.int32, (cl, cl), 1)
    diff = (row - col).astype(F32)
    decay = jnp.where(diff >= 0, jnp.exp(jnp.maximum(diff, 0.0) * lg), 0.0)
    scores = _dot_nt(q.astype(BF16), k.astype(BF16)) * decay
    o = _dot(scores.astype(BF16), v)
    idx = lax.broadcasted_iota(jnp.int32, (cl, 1), 0).astype(F32)
    s_prev = s_ref[...]
    o = o + _dot((q * jnp.exp((idx + 1.0) * lg)).astype(BF16), s_prev.astype(BF16))
    k_dec = k * jnp.exp((cl - 1.0 - idx) * lg)
    s_new = jnp.exp(cl * lg) * s_prev + _dot(k_dec.T.astype(BF16), v)
    s_ref[...] = s_new
    mu = jnp.mean(o, axis=-1, keepdims=True)
    var = jnp.mean(jnp.square(o - mu), axis=-1, keepdims=True)
    of = (o - mu) * lax.rsqrt(var + GN_EPS)
    g = g_ref[...]
    y_ref[...] = (g * jax.nn.sigmoid(g) * of).astype(y_ref.dtype)

    @pl.when(c == pl.num_programs(2) - 1)
    def _():
        so_ref[0, 0] = s_new


def _retention(qk, v, g, batch, heads, s0=None, chunk=256):
    m = qk.shape[0]
    dk, dv = qk.shape[1] // (2 * heads), v.shape[1] // heads
    seq = m // batch
    cl = _tile(seq, chunk)
    nc = seq // cl
    lg = jnp.log1p(-jnp.exp2(-5.0 - jnp.arange(heads, dtype=F32)))
    lg_tab = jnp.broadcast_to(lg[:, None, None], (heads, 8, LANES))
    tok = lambda b, h, c: (b * nc + c, h)
    in_specs = [pl.BlockSpec((1, 8, LANES), lambda b, h, c: (h, 0, 0)),
                pl.BlockSpec((cl, dk), tok), pl.BlockSpec((cl, dk), lambda b, h, c: (b * nc + c, heads + h)),
                pl.BlockSpec((cl, dv), tok), pl.BlockSpec((cl, dv), tok)]
    args = [lg_tab, qk, qk, v, g]
    if s0 is not None:
        in_specs.append(pl.BlockSpec((1, 1, dk, dv), lambda b, h, c: (b, h, 0, 0)))
        args.append(s0)
    return pl.pallas_call(
        functools.partial(_ret_kernel, has_s0=s0 is not None),
        grid=(batch, heads, nc),
        in_specs=in_specs,
        out_specs=[pl.BlockSpec((cl, dv), tok), pl.BlockSpec((1, 1, dk, dv), lambda b, h, c: (b, h, 0, 0))],
        out_shape=[jax.ShapeDtypeStruct((m, heads * dv), BF16), jax.ShapeDtypeStruct((batch, heads, dk, dv), F32)],
        scratch_shapes=[pltpu.VMEM((dk, dv), F32)],
        compiler_params=_cparams("parallel", "parallel", "arbitrary"),
        name="retention",
    )(*args)


def _mla_down_kernel(h_ref, wq_ref, wkv_ref, wr_ref, wrr_ref, gq_ref, gkv_ref, cos_ref, sin_ref,
                     cq_ref, ckv_ref, kr_ref):
    h = h_ref[...]
    cq_ref[...] = _rms(_dot(h, wq_ref[...]), gq_ref[...]).astype(cq_ref.dtype)
    ckv_ref[...] = _rms(_dot(h, wkv_ref[...]), gkv_ref[...])
    kr_ref[...] = _dot(h, wr_ref[...]) * cos_ref[...] + _dot(h, wrr_ref[...]) * sin_ref[...]


def _mla_down(h, wq, wkv, wr, wrr, gq, gkv, cos, sin, tm=512):
    m, d = h.shape
    ql, kvl, rd = wq.shape[1], wkv.shape[1], wr.shape[1]
    ltab = cos.shape[0]
    tm = _tile(math.gcd(m, ltab), tm)
    ntab = ltab // tm
    const = lambda i: (0, 0)
    row = lambda i: (i, 0)
    tab = lambda i: (i % ntab, 0)
    return pl.pallas_call(
        _mla_down_kernel,
        grid=(m // tm,),
        in_specs=[pl.BlockSpec((tm, d), row), pl.BlockSpec((d, ql), const), pl.BlockSpec((d, kvl), const),
                  pl.BlockSpec((d, rd), const), pl.BlockSpec((d, rd), const),
                  pl.BlockSpec((1, ql), const), pl.BlockSpec((1, kvl), const),
                  pl.BlockSpec((tm, rd), tab), pl.BlockSpec((tm, rd), tab)],
        out_specs=[pl.BlockSpec((tm, ql), row), pl.BlockSpec((tm, kvl), row), pl.BlockSpec((tm, rd), row)],
        out_shape=[jax.ShapeDtypeStruct((m, ql), BF16), jax.ShapeDtypeStruct((m, kvl), F32),
                   jax.ShapeDtypeStruct((m, rd), F32)],
        compiler_params=_cparams("parallel"),
        name="mla_down",
    )(h, wq, wkv, wr, wrr, gq.reshape(1, ql), gkv.reshape(1, kvl), cos, sin)


def _mla_q_kernel(cq_ref, wn_ref, wr_ref, wrr_ref, cos_ref, sin_ref, q_ref, *, scale):
    cq = cq_ref[...]
    qn = _dot(cq, wn_ref[...]) * scale
    nope = LANES
    for hh in range(q_ref.shape[0]):
        q_ref[hh, :, :nope] = qn[:, hh * LANES:(hh + 1) * LANES].astype(q_ref.dtype)
    a, b = _dot(cq, wr_ref[...]), _dot(cq, wrr_ref[...])
    cos, sin = cos_ref[...] * scale, sin_ref[...] * scale
    rd = q_ref.shape[2] - nope
    per = LANES // rd
    for p in range(a.shape[1] // LANES):
        r = a[:, p * LANES:(p + 1) * LANES] * cos + b[:, p * LANES:(p + 1) * LANES] * sin
        for s in range(per):
            q_ref[p * per + s, :, nope:] = r[:, s * rd:(s + 1) * rd].astype(q_ref.dtype)


def _mla_q(cq, wn, wr, wrr, cos, sin, heads, scale, tm=512):
    m, ql = cq.shape
    nope, rd = wn.shape[1] // heads, wr.shape[1] // heads
    assert nope == LANES and LANES % rd == 0
    ltab = cos.shape[0]
    tm = _tile(math.gcd(m, ltab), tm)
    ntab = ltab // tm
    const = lambda i: (0, 0)
    tab = lambda i: (i % ntab, 0)
    return pl.pallas_call(
        functools.partial(_mla_q_kernel, scale=scale),
        grid=(m // tm,),
        in_specs=[pl.BlockSpec((tm, ql), lambda i: (i, 0)), pl.BlockSpec(wn.shape, const),
                  pl.BlockSpec(wr.shape, const), pl.BlockSpec(wrr.shape, const),
                  pl.BlockSpec((tm, LANES), tab), pl.BlockSpec((tm, LANES), tab)],
        out_specs=pl.BlockSpec((heads, tm, nope + rd), lambda i: (0, i, 0)),
        out_shape=jax.ShapeDtypeStruct((heads, m, nope + rd), BF16),
        compiler_params=_cparams("parallel"),
        name="mla_q",
    )(cq, wn, wr, wrr, cos, sin)


def _mla_kv_kernel(ckv_ref, kr_ref, wk_ref, wv_ref, k_ref, v_ref):
    ckv = ckv_ref[...].astype(BF16)
    kn, v = _dot(ckv, wk_ref[...]), _dot(ckv, wv_ref[...])
    kr = kr_ref[...].astype(k_ref.dtype)
    nope = k_ref.shape[2] - kr.shape[1]
    vd = v_ref.shape[2]
    for hh in range(k_ref.shape[0]):
        k_ref[hh, :, :nope] = kn[:, hh * nope:(hh + 1) * nope].astype(k_ref.dtype)
        k_ref[hh, :, nope:] = kr
        v_ref[hh] = v[:, hh * vd:(hh + 1) * vd].astype(v_ref.dtype)


def _mla_kv(ckv, kr, wk, wv, heads, tm=512):
    m, kvl = ckv.shape
    rd = kr.shape[1]
    nope, vd = wk.shape[1] // heads, wv.shape[1] // heads
    tm = _tile(m, tm)
    const = lambda i: (0, 0)
    return pl.pallas_call(
        _mla_kv_kernel,
        grid=(m // tm,),
        in_specs=[pl.BlockSpec((tm, kvl), lambda i: (i, 0)), pl.BlockSpec((tm, rd), lambda i: (i, 0)),
                  pl.BlockSpec(wk.shape, const), pl.BlockSpec(wv.shape, const)],
        out_specs=[pl.BlockSpec((heads, tm, nope + rd), lambda i: (0, i, 0)),
                   pl.BlockSpec((heads, tm, vd), lambda i: (0, i, 0))],
        out_shape=[jax.ShapeDtypeStruct((heads, m, nope + rd), BF16), jax.ShapeDtypeStruct((heads, m, vd), BF16)],
        compiler_params=_cparams("parallel"),
        name="mla_kv",
    )(ckv, kr, wk, wv)


def _mla_flash_kernel(q_ref, k_ref, v_ref, o_ref, m_ref, l_ref, acc_ref):
    qi, ki = pl.program_id(1), pl.program_id(2)
    nh, tq, vd = acc_ref.shape
    tk = k_ref.shape[1]

    @pl.when(ki == 0)
    def _():
        m_ref[...] = jnp.full_like(m_ref, -jnp.inf)
        l_ref[...] = jnp.zeros_like(l_ref)
        acc_ref[...] = jnp.zeros_like(acc_ref)

    def sweep(diagonal):
        if diagonal:
            row_chunk = lax.broadcasted_iota(jnp.int32, (tq, tk), 0) // CHUNK
            col_chunk = lax.broadcasted_iota(jnp.int32, (tq, tk), 1) // CHUNK
            mask = col_chunk <= row_chunk

        def head(hh, carry):
            s = _dot_nt(q_ref[hh], k_ref[hh])
            if diagonal:
                s = jnp.where(mask, s, -jnp.inf)
            m_prev = m_ref[hh]
            m_new = jnp.maximum(m_prev, jnp.max(s, axis=-1, keepdims=True))
            alpha = jnp.exp(m_prev - m_new)
            p = jnp.concatenate([jnp.exp(s[:, c * LANES:(c + 1) * LANES] - m_new) for c in range(tk // LANES)],
                                axis=-1)
            l_ref[hh] = alpha * l_ref[hh] + jnp.sum(p, axis=-1, keepdims=True)
            acc_ref[hh] = alpha * acc_ref[hh] + _dot(p.astype(BF16), v_ref[hh])
            m_ref[hh] = m_new
            return carry

        lax.fori_loop(0, nh, head, 0, unroll=2)

    @pl.when(ki < qi)
    def _():
        sweep(False)

    @pl.when(ki == qi)
    def _():
        sweep(True)
        for hh in range(nh):
            o_ref[:, hh * vd:(hh + 1) * vd] = (acc_ref[hh] / l_ref[hh]).astype(o_ref.dtype)


def _mla_flash(q, k, v, batch, t=512):
    heads, m, qd = q.shape
    vd = v.shape[2]
    seq = m // batch
    t = _tile(seq, t)
    assert t % CHUNK == 0 and vd == LANES
    nq = seq // t
    qmap = lambda b, qi, ki: (0, b * nq + qi, 0)
    kmap = lambda b, qi, ki: (0, b * nq + jnp.minimum(ki, qi), 0)
    return pl.pallas_call(
        _mla_flash_kernel,
        grid=(batch, nq, nq),
        in_specs=[pl.BlockSpec((heads, t, qd), qmap), pl.BlockSpec((heads, t, qd), kmap),
                  pl.BlockSpec((heads, t, vd), kmap)],
        out_specs=pl.BlockSpec((t, heads * vd), lambda b, qi, ki: (b * nq + qi, 0)),
        out_shape=jax.ShapeDtypeStruct((m, heads * vd), BF16),
        scratch_shapes=[pltpu.VMEM((heads, t, LANES), F32), pltpu.VMEM((heads, t, LANES), F32),
                        pltpu.VMEM((heads, t, vd), F32)],
        compiler_params=_cparams("parallel", "parallel", "arbitrary"),
        name="mla_flash",
    )(q, k, v)


def _mla_sample_kernel(q_ref, ckv_ref, kr_ref, cckv_ref, ckr_ref, wuk_ref, wuv_ref, o_ref, ql_ref):
    nh, t, qd = q_ref.shape
    kvl = ckv_ref.shape[1]
    nope, vd = wuk_ref.shape[2], wuv_ref.shape[2]
    for hh in range(nh):
        ql_ref[hh] = _dot_nt(q_ref[hh, :, :nope], wuk_ref[hh]).astype(BF16)
    q_lat = ql_ref[...].reshape(nh * t, kvl)
    q_rope = q_ref[:, :, nope:].reshape(nh * t, qd - nope)
    c_old, r_old = cckv_ref[0].astype(BF16), ckr_ref[0].astype(BF16)
    c_new, r_new = ckv_ref[...].astype(BF16), kr_ref[...].astype(BF16)
    s_old = _dot_nt(q_lat, c_old) + _dot_nt(q_rope, r_old)
    s_new = _dot_nt(q_lat, c_new) + _dot_nt(q_rope, r_new)
    mx = jnp.maximum(jnp.max(s_old, axis=-1, keepdims=True), jnp.max(s_new, axis=-1, keepdims=True))
    p_old, p_new = jnp.exp(s_old - mx), jnp.exp(s_new - mx)
    denom = jnp.sum(p_old, axis=-1, keepdims=True) + jnp.sum(p_new, axis=-1, keepdims=True)
    o_lat = (_dot(p_old.astype(BF16), c_old) + _dot(p_new.astype(BF16), c_new)) / denom
    o_lat = o_lat.astype(BF16)
    for hh in range(nh):
        o_ref[:, hh * vd:(hh + 1) * vd] = _dot(o_lat[hh * t:(hh + 1) * t], wuv_ref[hh]).astype(o_ref.dtype)


def _mla_sample(q, ckv, kr, cache_ckv, cache_kr, wuk_t, wuv_t):
    heads, m, qd = q.shape
    batch, past, kvl = cache_ckv.shape
    t = m // batch
    rd = kr.shape[1]
    vd = wuv_t.shape[2]
    const3 = lambda b: (0, 0, 0)
    return pl.pallas_call(
        _mla_sample_kernel,
        grid=(batch,),
        in_specs=[pl.BlockSpec((heads, t, qd), lambda b: (0, b, 0)),
                  pl.BlockSpec((t, kvl), lambda b: (b, 0)), pl.BlockSpec((t, rd), lambda b: (b, 0)),
                  pl.BlockSpec((1, past, kvl), lambda b: (b, 0, 0)), pl.BlockSpec((1, past, rd), lambda b: (b, 0, 0)),
                  pl.BlockSpec(wuk_t.shape, const3), pl.BlockSpec(wuv_t.shape, const3)],
        out_specs=pl.BlockSpec((t, heads * vd), lambda b: (b, 0)),
        out_shape=jax.ShapeDtypeStruct((m, heads * vd), BF16),
        scratch_shapes=[pltpu.VMEM((heads, t, kvl), BF16)],
        compiler_params=_cparams("parallel"),
        name="mla_sample",
    )(q, ckv, kr, cache_ckv, cache_kr, wuk_t, wuv_t)


def _s5_kernel(*refs, has_h0, col_block):
    (x_ref, g_ref, wbr_ref, wbi_ref, ar_ref, ai_ref, cr_ref, ci_ref, d_ref), refs = refs[:9], refs[9:]
    if has_h0:
        (h0r_ref, h0i_ref), refs = refs[:2], refs[2:]
    y_ref, sr_ref, si_ref, br_ref, bi_ref, hr_ref, hi_ref = refs
    tt = pl.program_id(1)
    t, d = x_ref.shape
    n_state = br_ref.shape[1]
    n_gb, gin, gst = wbr_ref.shape

    @pl.when(tt == 0)
    def _():
        if has_h0:
            hr_ref[...] = h0r_ref[0]
            hi_ref[...] = h0i_ref[0]
        else:
            hr_ref[...] = jnp.zeros_like(hr_ref)
            hi_ref[...] = jnp.zeros_like(hi_ref)

    h = _rms(x_ref[...], g_ref[...])
    ub = h.astype(BF16)
    for gb in range(n_gb):
        u_blk = ub[:, gb * gin:(gb + 1) * gin]
        br_ref[:, gb * gst:(gb + 1) * gst] = _dot(u_blk, wbr_ref[gb])
        bi_ref[:, gb * gst:(gb + 1) * gst] = _dot(u_blk, wbi_ref[gb])

    for cb in range(n_state // col_block):
        sl = slice(cb * col_block, (cb + 1) * col_block)
        a_re, a_im = ar_ref[:, sl], ai_ref[:, sl]

        def step(s, carry):
            h_re, h_im = carry
            n_re = a_re * h_re - a_im * h_im + br_ref[pl.ds(s, 1), sl]
            n_im = a_re * h_im + a_im * h_re + bi_ref[pl.ds(s, 1), sl]
            br_ref[pl.ds(s, 1), sl] = n_re
            bi_ref[pl.ds(s, 1), sl] = n_im
            return n_re, n_im

        h_re, h_im = lax.fori_loop(0, t, step, (hr_ref[:, sl], hi_ref[:, sl]))
        hr_ref[:, sl] = h_re
        hi_ref[:, sl] = h_im

    for gb in range(n_gb):
        hs_re = br_ref[:, gb * gst:(gb + 1) * gst].astype(BF16)
        hs_im = bi_ref[:, gb * gst:(gb + 1) * gst].astype(BF16)
        cs = slice(gb * gin, (gb + 1) * gin)
        y = _dot(hs_re, cr_ref[gb]) - _dot(hs_im, ci_ref[gb]) + d_ref[:, cs] * h[:, cs]
        y_ref[:, cs] = _gelu_tanh(y).astype(y_ref.dtype)

    @pl.when(tt == pl.num_programs(1) - 1)
    def _():
        sr_ref[0] = hr_ref[...]
        si_ref[0] = hi_ref[...]


def _s5(x, g, wb_re, wb_im, a_re, a_im, wc_re, wc_im, d_skip, batch, h0=None, t=256, col_block=4096):
    m, d = x.shape
    seq = m // batch
    t = _tile(seq, t)
    nt = seq // t
    n_state = a_re.shape[1]
    col_block = _tile(n_state, col_block)
    const2 = lambda b, tt: (0, 0)
    const3 = lambda b, tt: (0, 0, 0)
    in_specs = [pl.BlockSpec((t, d), lambda b, tt: (b * nt + tt, 0)), pl.BlockSpec((1, d), const2),
                pl.BlockSpec(wb_re.shape, const3), pl.BlockSpec(wb_im.shape, const3),
                pl.BlockSpec((1, n_state), const2), pl.BlockSpec((1, n_state), const2),
                pl.BlockSpec(wc_re.shape, const3), pl.BlockSpec(wc_im.shape, const3),
                pl.BlockSpec((1, d), const2)]
    args = [x, g.reshape(1, d), wb_re, wb_im, a_re, a_im, wc_re, wc_im, d_skip.reshape(1, d)]
    if h0 is not None:
        in_specs += [pl.BlockSpec((1, 1, n_state), lambda b, tt: (b, 0, 0))] * 2
        args += [h0[0].reshape(batch, 1, n_state), h0[1].reshape(batch, 1, n_state)]
    state_spec = pl.BlockSpec((1, 1, n_state), lambda b, tt: (b, 0, 0))
    return pl.pallas_call(
        functools.partial(_s5_kernel, has_h0=h0 is not None, col_block=col_block),
        grid=(batch, nt),
        in_specs=in_specs,
        out_specs=[pl.BlockSpec((t, d), lambda b, tt: (b * nt + tt, 0)), state_spec, state_spec],
        out_shape=[jax.ShapeDtypeStruct((m, d), BF16), jax.ShapeDtypeStruct((batch, 1, n_state), F32),
                   jax.ShapeDtypeStruct((batch, 1, n_state), F32)],
        scratch_shapes=[pltpu.VMEM((t, n_state), F32), pltpu.VMEM((t, n_state), F32),
                        pltpu.VMEM((1, n_state), F32), pltpu.VMEM((1, n_state), F32)],
        compiler_params=_cparams("parallel", "arbitrary"),
        name="s5_scan",
    )(*args)


def _s5_params(a_re, a_im, log_dt, b_re, b_im, c_re, c_im, groups_per_block=8):
    g, p, gc = b_re.shape
    step = jnp.exp(log_dt)[:, None]
    xr, xi = a_re * step, a_im * step
    ab_re, ab_im = jnp.exp(xr) * jnp.cos(xi), jnp.exp(xr) * jnp.sin(xi)
    m1_re = jnp.expm1(xr) * jnp.cos(xi) - 2.0 * jnp.square(jnp.sin(0.5 * xi))
    den = a_re * a_re + a_im * a_im
    f_re = (m1_re * a_re + ab_im * a_im) / den
    f_im = (ab_im * a_re - m1_re * a_im) / den
    bb_re = f_re[..., None] * b_re - f_im[..., None] * b_im
    bb_im = f_re[..., None] * b_im + f_im[..., None] * b_re
    gpb = groups_per_block
    eye = jnp.eye(gpb, dtype=F32)

    def blockdiag_b(w):
        w = w.reshape(g // gpb, gpb, p, gc)
        return jnp.einsum('bgpc,gh->bgchp', w, eye).reshape(g // gpb, gpb * gc, gpb * p).astype(BF16)

    def blockdiag_c(w):
        w = w.reshape(g // gpb, gpb, gc, p)
        return jnp.einsum('bgcp,gh->bgphc', w, eye).reshape(g // gpb, gpb * p, gpb * gc).astype(BF16)

    return (blockdiag_b(bb_re), blockdiag_b(bb_im), ab_re.reshape(1, g * p), ab_im.reshape(1, g * p),
            blockdiag_c(c_re), blockdiag_c(c_im))


def _sb_block(q, k, v, mask, tri, r_prev, scale):
    z = _dot_nt(q, k) * scale
    sp = _softplus(z)
    if mask is not None:
        sp = jnp.where(mask, sp, 0.0)
    hi = sp.astype(BF16)
    lo = (sp - hi.astype(F32)).astype(BF16)
    csum = _dot(hi, tri) + _dot(lo, tri)
    log_w = z - csum
    if mask is not None:
        log_w = jnp.where(mask, log_w, -jnp.inf)
    n_keys = z.shape[1]
    if n_keys % LANES:
        w = jnp.exp(log_w - r_prev[:, :n_keys])
    else:
        w = jnp.concatenate([jnp.exp(log_w[:, c * LANES:(c + 1) * LANES] - r_prev) for c in range(n_keys // LANES)],
                            axis=-1)
    return _dot(w.astype(BF16), v), r_prev + jnp.sum(sp, axis=-1, keepdims=True)


def _tri(n):
    row = lax.broadcasted_iota(jnp.int32, (n, n), 0)
    col = lax.broadcasted_iota(jnp.int32, (n, n), 1)
    return (row >= col).astype(BF16)


def _sb_flash_kernel(q_ref, k_hbm, v_hbm, o_ref, kbuf, vbuf, sem, r_ref, acc_ref, *, scale, seq):
    b, qi = pl.program_id(0), pl.program_id(1)
    nh, t, hd = acc_ref.shape
    row0 = b * seq + qi * t

    def copies(kj, slot):
        start = pl.multiple_of(row0 - kj * t, t)
        return (pltpu.make_async_copy(k_hbm.at[:, pl.ds(start, t), :], kbuf.at[slot], sem.at[0, slot]),
                pltpu.make_async_copy(v_hbm.at[:, pl.ds(start, t), :], vbuf.at[slot], sem.at[1, slot]))

    for cp in copies(0, 0):
        cp.start()
    r_ref[...] = jnp.zeros_like(r_ref)
    acc_ref[...] = jnp.zeros_like(acc_ref)
    row = lax.broadcasted_iota(jnp.int32, (t, t), 0)
    col = lax.broadcasted_iota(jnp.int32, (t, t), 1)
    tri = _tri(t)

    def sweep(carry):
        kj, _ = carry
        slot = kj % 2
        for cp in copies(kj, slot):
            cp.wait()

        @pl.when(kj < qi)
        def _():
            for cp in copies(kj + 1, 1 - slot):
                cp.start()

        mask = (col < row) | (kj > 0)

        def head(hh, r_min):
            o, r = _sb_block(q_ref[hh], kbuf[slot, hh], vbuf[slot, hh], mask, tri, r_ref[hh], scale)
            acc_ref[hh] += o
            r_ref[hh] = r
            return jnp.minimum(r_min, jnp.min(r))

        r_min = lax.fori_loop(0, nh, head, jnp.float32(jnp.inf), unroll=2)
        return kj + 1, (r_min > SB_EXIT_SUM).astype(jnp.int32)

    kj_end, _ = lax.while_loop(lambda c: (c[0] <= qi) & (c[1] == 0), sweep, (jnp.int32(0), jnp.int32(0)))

    @pl.when(kj_end <= qi)
    def _():
        for cp in copies(kj_end, kj_end % 2):
            cp.wait()

    for hh in range(nh):
        o_ref[:, hh * hd:(hh + 1) * hd] = acc_ref[hh].astype(o_ref.dtype)


def _sb_flash(q, k, v, batch, scale, t=256):
    heads, m, hd = q.shape
    seq = m // batch
    t = _tile(seq, t)
    nq = seq // t
    return pl.pallas_call(
        functools.partial(_sb_flash_kernel, scale=scale, seq=seq),
        grid=(batch, nq),
        in_specs=[pl.BlockSpec((heads, t, hd), lambda b, qi: (0, b * nq + qi, 0)),
                  pl.BlockSpec(memory_space=pl.ANY), pl.BlockSpec(memory_space=pl.ANY)],
        out_specs=pl.BlockSpec((t, heads * hd), lambda b, qi: (b * nq + qi, 0)),
        out_shape=jax.ShapeDtypeStruct((m, heads * hd), BF16),
        scratch_shapes=[pltpu.VMEM((2, heads, t, hd), BF16), pltpu.VMEM((2, heads, t, hd), BF16),
                        pltpu.SemaphoreType.DMA((2, 2)),
                        pltpu.VMEM((heads, t, LANES), F32), pltpu.VMEM((heads, t, hd), F32)],
        compiler_params=_cparams("parallel", "parallel"),
        name="sb_flash",
    )(q, k, v)


def _sb_sample_kernel(q_ref, kn_ref, vn_ref, kc_hbm, vc_hbm, o_ref, kbuf, vbuf, sem, r_ref, acc_ref, *,
                      scale, layer, blk):
    b = pl.program_id(0)
    nh, t, hd = acc_ref.shape
    nblk = kc_hbm.shape[2] // blk

    def copies(kb, slot):
        start = pl.multiple_of((nblk - 1 - kb) * blk, blk)
        cps = []
        for hh in range(nh):
            cps.append(pltpu.make_async_copy(kc_hbm.at[layer, b, pl.ds(start, blk), hh], kbuf.at[slot, hh],
                                             sem.at[0, slot]))
            cps.append(pltpu.make_async_copy(vc_hbm.at[layer, b, pl.ds(start, blk), hh], vbuf.at[slot, hh],
                                             sem.at[1, slot]))
        return cps

    for cp in copies(0, 0):
        cp.start()
    row = lax.broadcasted_iota(jnp.int32, (t, t), 0)
    col = lax.broadcasted_iota(jnp.int32, (t, t), 1)
    own_mask, own_tri = col < row, _tri(t)

    def own(hh, carry):
        o, r = _sb_block(q_ref[hh], kn_ref[hh], vn_ref[hh], own_mask, own_tri, jnp.zeros((t, LANES), F32), scale)
        acc_ref[hh] = o
        r_ref[hh] = r
        return carry

    lax.fori_loop(0, nh, own, 0)
    tri = _tri(blk)

    def sweep(carry):
        kb, _ = carry
        slot = kb % 2
        for cp in copies(kb, slot):
            cp.wait()

        @pl.when(kb + 1 < nblk)
        def _():
            for cp in copies(kb + 1, 1 - slot):
                cp.start()

        def head(hh, r_min):
            o, r = _sb_block(q_ref[hh], kbuf[slot, hh].astype(BF16), vbuf[slot, hh].astype(BF16), None, tri,
                             r_ref[hh], scale)
            acc_ref[hh] += o
            r_ref[hh] = r
            return jnp.minimum(r_min, jnp.min(r))

        r_min = lax.fori_loop(0, nh, head, jnp.float32(jnp.inf))
        return kb + 1, (r_min > SB_EXIT_SUM).astype(jnp.int32)

    kb_end, _ = lax.while_loop(lambda c: (c[0] < nblk) & (c[1] == 0), sweep, (jnp.int32(0), jnp.int32(0)))

    @pl.when(kb_end < nblk)
    def _():
        for cp in copies(kb_end, kb_end % 2):
            cp.wait()

    for hh in range(nh):
        o_ref[:, hh * hd:(hh + 1) * hd] = acc_ref[hh].astype(o_ref.dtype)


def _sb_sample(q, k_new, v_new, cache_k, cache_v, layer, scale, blk=256):
    heads, m, hd = q.shape
    batch, past = cache_k.shape[1], cache_k.shape[2]
    t = m // batch
    blk = _tile(past, blk)
    new = lambda b: (0, b, 0)
    return pl.pallas_call(
        functools.partial(_sb_sample_kernel, scale=scale, layer=layer, blk=blk),
        grid=(batch,),
        in_specs=[pl.BlockSpec((heads, t, hd), new), pl.BlockSpec((heads, t, hd), new),
                  pl.BlockSpec((heads, t, hd), new),
                  pl.BlockSpec(memory_space=pl.ANY), pl.BlockSpec(memory_space=pl.ANY)],
        out_specs=pl.BlockSpec((t, heads * hd), lambda b: (b, 0)),
        out_shape=jax.ShapeDtypeStruct((m, heads * hd), BF16),
        scratch_shapes=[pltpu.VMEM((2, heads, blk, hd), F32), pltpu.VMEM((2, heads, blk, hd), F32),
                        pltpu.SemaphoreType.DMA((2, 2)),
                        pltpu.VMEM((heads, t, LANES), F32), pltpu.VMEM((heads, t, hd), F32)],
        compiler_params=_cparams("parallel"),
        name="sb_sample",
    )(q, k_new, v_new, cache_k, cache_v)


def _rope_tables(pos, d, reps):
    half = d // 2
    inv = ROPE_BASE ** (-jnp.arange(half, dtype=F32) * 2.0 / d)
    ang = pos.astype(F32)[:, None] * inv[None, :]
    return jnp.tile(jnp.cos(ang), (1, reps)), jnp.tile(jnp.sin(ang), (1, reps))


def _rot_half_cols(w, d):
    k, n = w.shape
    w = w.reshape(k, n // d, d)
    half = d // 2
    return jnp.concatenate([-w[..., half:], w[..., :half]], axis=-1).reshape(k, n)


def kernel(x_prompt, x_sample, state_ret, cache_mla_ckv, cache_mla_krope, state_s5_re, state_s5_im, cache_sb_k, cache_sb_v, state_ffn_conv, norm_gains, ret_w_in, ret_w_out, mla_w_down, mla_g_q, mla_g_kv, mla_w_uq, mla_w_uk, mla_w_uv, mla_w_o, s5_a_re, s5_a_im, s5_log_dt, s5_b_re, s5_b_im, s5_c_re, s5_c_im, s5_d, s5_w_glu, s5_b_glu, sb_w_qkv, sb_w_o, ffn_w_up, ffn_conv_w, ffn_conv_b, ffn_w_down):
    n_p, seq_p, d = x_prompt.shape
    n_s, seq_s, _ = x_sample.shape
    depth = norm_gains.shape[0]
    n_mixers = 4
    streams = (dict(batch=n_p, seq=seq_p, pos=jnp.arange(seq_p)),
               dict(batch=n_s, seq=seq_s, pos=None))
    xs = [x_prompt.reshape(n_p * seq_p, d), x_sample.reshape(n_s * seq_s, d)]
    hs = [_rmsnorm(x, norm_gains[0, 0]) for x in xs]
    outs = {name: ([], []) for name in ('ret', 'ckv', 'kr', 's5re', 's5im', 'sbk', 'sbv', 'conv')}

    past_len = cache_mla_ckv.shape[2]

    def positions(si):
        if si == 0:
            return jnp.arange(seq_p)
        return jnp.tile(past_len + jnp.arange(seq_s), n_s)

    for i in range(depth):
        j, kind = i // n_mixers, i % n_mixers
        g = norm_gains[i]
        g_next = norm_gains[i + 1, 0] if i + 1 < depth else g[0]
        emit_next = i + 1 < depth and (i + 1) % n_mixers != 2
        ffn_up = _block_cols(ffn_w_up[i], _tile(ffn_w_down.shape[1], FFN_COLS))
        ffn_down = ffn_w_down[i].astype(BF16)
        for si, st in enumerate(streams):
            x, h, batch = xs[si], hs[si], st['batch']
            if kind == 0:
                heads, dk, dv = state_ret.shape[2], state_ret.shape[3], state_ret.shape[4]
                qk_w, v_w = heads * dk, heads * dv
                w_in = ret_w_in[j].astype(BF16)
                cos, sin = _rope_tables(positions(si), dk, 1)
                qk = _linear_rope(h, w_in, 2 * qk_w, cos, sin, dk, qk_w, dk ** -0.5)
                v, = _linear(h, w_in, 2 * qk_w, v_w, (BF16,), name="ret_v")
                gate, = _linear(h, w_in, 2 * qk_w + v_w, v_w, (F32,), name="ret_g")
                s0 = state_ret[j] if si == 1 else None
                y, s1 = _retention(qk, v, gate, batch, heads, s0)
                outs['ret'][si].append(s1)
                x, h = _outproj(y, ret_w_out[j].astype(BF16), x, g[1], g[2], name="ret_outproj")
            elif kind == 1:
                ql, kvl = mla_g_q.shape[1], mla_g_kv.shape[1]
                heads, nope = mla_w_uk.shape[2], mla_w_uk.shape[3]
                rd, vd = cache_mla_krope.shape[3], mla_w_uv.shape[3]
                past = cache_mla_ckv.shape[2]
                scale = (nope + rd) ** -0.5
                w_down = mla_w_down[j]
                w_kr = w_down[:, ql + kvl:]
                pos = positions(si)
                cos_k, sin_k = _rope_tables(pos, rd, 2)
                cos_q, sin_q = _rope_tables(pos, rd, 2 * LANES // rd)
                cq, ckv, kr = _mla_down(h, w_down[:, :ql].astype(BF16), w_down[:, ql:ql + kvl].astype(BF16),
                                        w_kr.astype(BF16), _rot_half_cols(w_kr, rd).astype(BF16),
                                        mla_g_q[j], mla_g_kv[j], cos_k, sin_k)
                w_uq = mla_w_uq[j].reshape(ql, heads, nope + rd)
                w_qn = w_uq[:, :, :nope].reshape(ql, heads * nope)
                w_qr = w_uq[:, :, nope:].reshape(ql, heads * rd)
                q = _mla_q(cq, w_qn.astype(BF16), w_qr.astype(BF16), _rot_half_cols(w_qr, rd).astype(BF16),
                           cos_q, sin_q, heads, scale)
                outs['ckv'][si].append(ckv.reshape(batch, st['seq'], kvl))
                outs['kr'][si].append(kr.reshape(batch, st['seq'], rd))
                if si == 0:
                    k, v = _mla_kv(ckv, kr, mla_w_uk[j].reshape(kvl, heads * nope).astype(BF16),
                                   mla_w_uv[j].reshape(kvl, heads * vd).astype(BF16), heads)
                    o = _mla_flash(q, k, v, batch)
                else:
                    o = _mla_sample(q, ckv, kr, cache_mla_ckv[j], cache_mla_krope[j],
                                    jnp.transpose(mla_w_uk[j], (1, 0, 2)).astype(BF16),
                                    jnp.transpose(mla_w_uv[j], (1, 0, 2)).astype(BF16))
                x, h = _outproj(o, mla_w_o[j].astype(BF16), x, g[1], g[2], name="mla_outproj")
            elif kind == 2:
                params = _s5_params(s5_a_re[j], s5_a_im[j], s5_log_dt[j], s5_b_re[j], s5_b_im[j],
                                    s5_c_re[j], s5_c_im[j])
                h0 = (state_s5_re[j], state_s5_im[j]) if si == 1 else None
                y, s_re, s_im = _s5(x, g[0], *params, s5_d[j], batch, h0)
                outs['s5re'][si].append(s_re.reshape(batch, *s5_a_re.shape[1:]))
                outs['s5im'][si].append(s_im.reshape(batch, *s5_a_re.shape[1:]))
                x, h = _glu_outproj(y, s5_w_glu[j].astype(BF16), s5_b_glu[j], x, g[1], g[2])
            else:
                heads, hd = cache_sb_k.shape[3], cache_sb_k.shape[4]
                past = cache_sb_k.shape[2]
                w_qkv = sb_w_qkv[j].astype(BF16)
                scale = hd ** -0.5
                q, = _linear(h, w_qkv, 0, heads * hd, (BF16,), (True,), name="sb_q")
                k32, k16 = _linear(h, w_qkv, heads * hd, heads * hd, (F32, BF16), (False, True), name="sb_k")
                v32, v16 = _linear(h, w_qkv, 2 * heads * hd, heads * hd, (F32, BF16), (False, True), name="sb_v")
                outs['sbk'][si].append(k32.reshape(batch, st['seq'], heads, hd))
                outs['sbv'][si].append(v32.reshape(batch, st['seq'], heads, hd))
                if si == 0:
                    o = _sb_flash(q, k16, v16, batch, scale)
                else:
                    o = _sb_sample(q, k16, v16, cache_sb_k, cache_sb_v, j, scale)
                x, h = _outproj(o, sb_w_o[j].astype(BF16), x, g[1], g[2], name="sb_outproj")
            x, h, conv = _conv_ffn(h, x, ffn_up, ffn_conv_w[i], ffn_conv_b[i], ffn_down, g[3], g_next, st['seq'],
                                   state=state_ffn_conv[i] if si == 1 else None, emit_h=emit_next)
            outs['conv'][si].append(conv)
            xs[si], hs[si] = x, h

    stack = lambda name: (jnp.stack(outs[name][0]), jnp.stack(outs[name][1]))
    return (xs[0].reshape(x_prompt.shape), xs[1].reshape(x_sample.shape),
            *stack('ret'), *stack('ckv'), *stack('kr'), *stack('s5re'), *stack('s5im'),
            *stack('sbk'), *stack('sbv'), *stack('conv'))
```

```python
import functools
import math

import jax
import jax.numpy as jnp
from jax import lax
from jax.experimental import pallas as pl
from jax.experimental.pallas import tpu as pltpu

F32 = jnp.float32
BF16 = jnp.bfloat16
NORM_EPS = 1e-6
GN_EPS = 1e-5
ROPE_BASE = 10000.0
CHUNK = 64
CONV_W = 3
LANES = 128
VMEM_LIMIT = 56 * 1024 * 1024
SB_EXIT_SUM = 104.0
FFN_COLS = 512


def _cparams(*sem):
    return pltpu.CompilerParams(dimension_semantics=sem, vmem_limit_bytes=VMEM_LIMIT)


def _tile(n, pref):
    t = min(n, pref)
    while n % t:
        t -= 1
    return t


def _dot(a, b):
    return jnp.dot(a, b, preferred_element_type=F32)


def _dot_nt(a, b):
    return lax.dot_general(a, b, (((1,), (1,)), ((), ())), preferred_element_type=F32)


def _rms(x, g):
    return x * lax.rsqrt(jnp.mean(x * x, axis=-1, keepdims=True) + NORM_EPS) * g


def _gelu_tanh(x):
    k = -2.0 * math.sqrt(2.0 / math.pi) * math.log2(math.e)
    return x / (1.0 + jnp.exp2(x * (k + (k * 0.044715) * (x * x))))


def _softplus(z):
    return jnp.maximum(z, 0.0) + jnp.log(1.0 + jnp.exp(-jnp.abs(z)))


def _rmsnorm_kernel(x_ref, g_ref, o_ref):
    o_ref[...] = _rms(x_ref[...], g_ref[...]).astype(o_ref.dtype)


def _rmsnorm(x, g, out_dtype=BF16, tm=512):
    m, d = x.shape
    tm = _tile(m, tm)
    return pl.pallas_call(
        _rmsnorm_kernel,
        grid=(m // tm,),
        in_specs=[pl.BlockSpec((tm, d), lambda i: (i, 0)), pl.BlockSpec((1, d), lambda i: (0, 0))],
        out_specs=pl.BlockSpec((tm, d), lambda i: (i, 0)),
        out_shape=jax.ShapeDtypeStruct((m, d), out_dtype),
        compiler_params=_cparams("parallel"),
        name="rmsnorm",
    )(x, g.reshape(1, d))


def _linear_kernel(a_ref, w_ref, *o_refs, head_major):
    acc = _dot(a_ref[...].astype(BF16), w_ref[...])
    for o_ref in o_refs:
        if head_major and len(o_ref.shape) == 3:
            for hh in range(o_ref.shape[0]):
                o_ref[hh] = acc[:, hh * LANES:(hh + 1) * LANES].astype(o_ref.dtype)
        else:
            o_ref[...] = acc.astype(o_ref.dtype)


def _linear(a, w, col_off, n_out, out_dtypes, head_major=(), tm=1024, tn=512, name="linear"):
    m, k = a.shape
    tm, tn = _tile(m, tm), _tile(n_out, tn)
    assert col_off % tn == 0 and tn % LANES == 0
    joff = col_off // tn
    out_specs, out_shape = [], []
    for dt, hm in zip(out_dtypes, head_major or (False,) * len(out_dtypes)):
        if hm:
            out_specs.append(pl.BlockSpec((tn // LANES, tm, LANES), lambda i, j: (j, i, 0)))
            out_shape.append(jax.ShapeDtypeStruct((n_out // LANES, m, LANES), dt))
        else:
            out_specs.append(pl.BlockSpec((tm, tn), lambda i, j: (i, j)))
            out_shape.append(jax.ShapeDtypeStruct((m, n_out), dt))
    return pl.pallas_call(
        functools.partial(_linear_kernel, head_major=bool(head_major)),
        grid=(m // tm, n_out // tn),
        in_specs=[pl.BlockSpec((tm, k), lambda i, j: (i, 0)),
                  pl.BlockSpec((k, tn), lambda i, j: (0, joff + j))],
        out_specs=out_specs,
        out_shape=out_shape,
        compiler_params=_cparams("parallel", "arbitrary"),
        name=name,
    )(a, w)


def _linear_rope_kernel(a_ref, w_ref, cos_ref, sin_ref, o_ref, *, dk, n_q_tiles, k_scale):
    acc = _dot(a_ref[...], w_ref[...])
    scale = jnp.where(pl.program_id(1) >= n_q_tiles, k_scale, 1.0).astype(F32)
    cos, sin = cos_ref[...] * scale, sin_ref[...] * scale
    half = dk // 2
    for hh in range(acc.shape[1] // dk):
        x1 = acc[:, hh * dk:hh * dk + half]
        x2 = acc[:, hh * dk + half:(hh + 1) * dk]
        o_ref[:, hh * dk:hh * dk + half] = x1 * cos - x2 * sin
        o_ref[:, hh * dk + half:(hh + 1) * dk] = x1 * sin + x2 * cos


def _linear_rope(a, w, n_out, cos, sin, dk, n_q, k_scale, tm=1024, tn=512):
    m, k = a.shape
    ltab = cos.shape[0]
    tm = _tile(math.gcd(m, ltab), tm)
    tn = _tile(n_out, tn)
    ntab = ltab // tm
    return pl.pallas_call(
        functools.partial(_linear_rope_kernel, dk=dk, n_q_tiles=n_q // tn, k_scale=k_scale),
        grid=(m // tm, n_out // tn),
        in_specs=[pl.BlockSpec((tm, k), lambda i, j: (i, 0)),
                  pl.BlockSpec((k, tn), lambda i, j: (0, j)),
                  pl.BlockSpec((tm, dk // 2), lambda i, j: (i % ntab, 0)),
                  pl.BlockSpec((tm, dk // 2), lambda i, j: (i % ntab, 0))],
        out_specs=pl.BlockSpec((tm, tn), lambda i, j: (i, j)),
        out_shape=jax.ShapeDtypeStruct((m, n_out), F32),
        compiler_params=_cparams("parallel", "arbitrary"),
        name="ret_qk_rope",
    )(a, w, cos, sin)


def _outproj_kernel(a_ref, w_ref, x_ref, gp_ref, gn_ref, xo_ref, *rest, emit_h):
    acc_ref = rest[-1]
    kk = pl.program_id(1)

    @pl.when(kk == 0)
    def _():
        acc_ref[...] = jnp.zeros_like(acc_ref)

    acc_ref[...] += _dot(a_ref[...].astype(BF16), w_ref[...])

    @pl.when(kk == pl.num_programs(1) - 1)
    def _():
        xn = x_ref[...] + _rms(acc_ref[...], gp_ref[...])
        xo_ref[...] = xn
        if emit_h:
            rest[0][...] = _rms(xn, gn_ref[...]).astype(BF16)


def _outproj(a, w, x, g_post, g_next, emit_h=True, tm=512, tk=512, name="outproj"):
    m, k = a.shape
    d = w.shape[1]
    tm, tk = _tile(m, tm), _tile(k, tk)
    out_specs = [pl.BlockSpec((tm, d), lambda i, kk: (i, 0))]
    out_shape = [jax.ShapeDtypeStruct((m, d), F32)]
    if emit_h:
        out_specs.append(pl.BlockSpec((tm, d), lambda i, kk: (i, 0)))
        out_shape.append(jax.ShapeDtypeStruct((m, d), BF16))
    res = pl.pallas_call(
        functools.partial(_outproj_kernel, emit_h=emit_h),
        grid=(m // tm, k // tk),
        in_specs=[pl.BlockSpec((tm, tk), lambda i, kk: (i, kk)),
                  pl.BlockSpec((tk, d), lambda i, kk: (kk, 0)),
                  pl.BlockSpec((tm, d), lambda i, kk: (i, 0)),
                  pl.BlockSpec((1, d), lambda i, kk: (0, 0)),
                  pl.BlockSpec((1, d), lambda i, kk: (0, 0))],
        out_specs=out_specs,
        out_shape=out_shape,
        scratch_shapes=[pltpu.VMEM((tm, d), F32)],
        compiler_params=_cparams("parallel", "arbitrary"),
        name=name,
    )(a, w, x, g_post.reshape(1, d), g_next.reshape(1, d))
    return (res[0], res[1]) if emit_h else (res[0], None)


def _glu_outproj_kernel(a_ref, wv_ref, wg_ref, bv_ref, bg_ref, x_ref, gp_ref, gn_ref, xo_ref, ho_ref,
                        accv_ref, accg_ref):
    kk = pl.program_id(1)

    @pl.when(kk == 0)
    def _():
        accv_ref[...] = jnp.zeros_like(accv_ref)
        accg_ref[...] = jnp.zeros_like(accg_ref)

    a = a_ref[...]
    accv_ref[...] += _dot(a, wv_ref[...])
    accg_ref[...] += _dot(a, wg_ref[...])

    @pl.when(kk == pl.num_programs(1) - 1)
    def _():
        mix = (accv_ref[...] + bv_ref[...]) * jax.nn.sigmoid(accg_ref[...] + bg_ref[...])
        xn = x_ref[...] + _rms(mix, gp_ref[...])
        xo_ref[...] = xn
        ho_ref[...] = _rms(xn, gn_ref[...]).astype(BF16)


def _glu_outproj(a, w, b, x, g_post, g_next, tm=512, tk=512):
    m, k = a.shape
    d = w.shape[1] // 2
    tm, tk = _tile(m, tm), _tile(k, tk)
    row = lambda i, kk: (i, 0)
    const = lambda i, kk: (0, 0)
    return pl.pallas_call(
        _glu_outproj_kernel,
        grid=(m // tm, k // tk),
        in_specs=[pl.BlockSpec((tm, tk), lambda i, kk: (i, kk)),
                  pl.BlockSpec((tk, d), lambda i, kk: (kk, 0)),
                  pl.BlockSpec((tk, d), lambda i, kk: (kk, 1)),
                  pl.BlockSpec((1, d), lambda i, kk: (0, 0)),
                  pl.BlockSpec((1, d), lambda i, kk: (0, 1)),
                  pl.BlockSpec((tm, d), row), pl.BlockSpec((1, d), const), pl.BlockSpec((1, d), const)],
        out_specs=[pl.BlockSpec((tm, d), row), pl.BlockSpec((tm, d), row)],
        out_shape=[jax.ShapeDtypeStruct((m, d), F32), jax.ShapeDtypeStruct((m, d), BF16)],
        scratch_shapes=[pltpu.VMEM((tm, d), F32), pltpu.VMEM((tm, d), F32)],
        compiler_params=_cparams("parallel", "arbitrary"),
        name="s5_glu_outproj",
    )(a, w, w, b.reshape(1, 2 * d), b.reshape(1, 2 * d), x, g_post.reshape(1, d), g_next.reshape(1, d))


def _ffn_kernel(*refs, n_seq, has_state, emit_h, tiles_per_seq, nf):
    (h_ref, x_ref, wv_ref, wg_ref, cwv_ref, cwg_ref, cbv_ref, cbg_ref, wd_ref, gp_ref, gn_ref), refs = refs[:11], refs[11:]
    if has_state:
        (stv_ref, stg_ref), refs = refs[:2], refs[2:]
    xo_ref, refs = refs[0], refs[1:]
    if emit_h:
        ho_ref, refs = refs[0], refs[1:]
    cv_ref, cg_ref, acc_ref, act_even_ref, act_odd_ref = refs[:5]
    carry_refs = refs[5:]
    i, j = pl.program_id(0), pl.program_id(1)
    tm, tf = h_ref.shape[0], wv_ref.shape[1]
    seq = tm // n_seq

    if not has_state:
        @pl.when((i == 0) & (j == 0))
        def _():
            for carry_ref in carry_refs[:2]:
                carry_ref[...] = jnp.zeros_like(carry_ref)

    def conv_with_state(w_ref, cw_ref, cb_ref, st_ref, c_out_ref):
        up = _dot(h_ref[...], w_ref[...]).reshape(n_seq, seq, tf)
        prev1, prev2 = st_ref[:, 1:2, :], st_ref[:, 0:1, :]
        pos = lax.broadcasted_iota(jnp.int32, up.shape, 1)
        p1 = jnp.where(pos == 0, prev1, pltpu.roll(up, 1, axis=1))
        p2 = jnp.where(pos == 0, prev2, jnp.where(pos == 1, prev1, pltpu.roll(up, 2, axis=1)))
        c_out_ref[...] = up[:, seq - 2:, :]
        cw = cw_ref[...]
        c = cb_ref[...][None] + cw[0:1][None] * p2 + cw[1:2][None] * p1 + cw[2:3][None] * up
        return c.reshape(tm, tf)

    def conv_with_carry(w_ref, cw_ref, cb_ref, carry_ref, ext_ref, c_out_ref):
        up = _dot(h_ref[...], w_ref[...])
        ext_ref[0:8, :] = jnp.where(i % tiles_per_seq != 0, carry_ref[j], 0.0)
        ext_ref[8:, :] = up
        carry_ref[j] = up[tm - 8:, :]
        c_out_ref[0] = up[tm - 2:, :]
        cw = cw_ref[...]
        return cb_ref[...] + cw[0:1] * ext_ref[6:6 + tm, :] + cw[1:2] * ext_ref[7:7 + tm, :] + cw[2:3] * up

    def activation():
        if has_state:
            val = conv_with_state(wv_ref, cwv_ref, cbv_ref, stv_ref, cv_ref)
            gate = conv_with_state(wg_ref, cwg_ref, cbg_ref, stg_ref, cg_ref)
        else:
            val = conv_with_carry(wv_ref, cwv_ref, cbv_ref, carry_refs[0], carry_refs[2], cv_ref)
            gate = conv_with_carry(wg_ref, cwg_ref, cbg_ref, carry_refs[1], carry_refs[3], cg_ref)
        return (_gelu_tanh(gate) * val).astype(BF16)

    @pl.when(j == 0)
    def _():
        acc_ref[...] = jnp.zeros_like(acc_ref)
        act_odd_ref[...] = jnp.zeros_like(act_odd_ref)

    def step(act_new_ref, act_old_ref):
        act_new_ref[...] = activation()
        acc_ref[...] += _dot(act_old_ref[...], wd_ref[...])

    @pl.when((j < nf) & (j % 2 == 0))
    def _():
        step(act_even_ref, act_odd_ref)

    @pl.when((j < nf) & (j % 2 == 1))
    def _():
        step(act_odd_ref, act_even_ref)

    @pl.when(j == nf)
    def _():
        last = act_even_ref if (nf - 1) % 2 == 0 else act_odd_ref
        acc_ref[...] += _dot(last[...], wd_ref[...])
        xn = x_ref[...] + _rms(acc_ref[...], gp_ref[...])
        xo_ref[...] = xn
        if emit_h:
            ho_ref[...] = _rms(xn, gn_ref[...]).astype(BF16)


def _conv_ffn(h, x, w_up, conv_w, conv_b, w_down, g_post, g_next, seq_len, state=None, emit_h=True, tm=512):
    m, d = h.shape
    d_ff = w_down.shape[0]
    tf = _tile(d_ff, FFN_COLS)
    nf = d_ff // tf
    has_state = state is not None
    if has_state:
        tm = _tile(m // seq_len, max(tm // seq_len, 1)) * seq_len
        n_seq, tiles_per_seq = tm // seq_len, 1
    else:
        tm = _tile(seq_len, tm)
        n_seq, tiles_per_seq = 1, seq_len // tm
    n_seqs = m // seq_len
    row = lambda i, j: (i, 0)
    const = lambda i, j: (0, 0)
    up = lambda j: jnp.minimum(j, nf - 1)
    val_cols = lambda i, j: (0, up(j))
    gate_cols = lambda i, j: (0, nf + up(j))
    in_specs = [pl.BlockSpec((tm, d), row), pl.BlockSpec((tm, d), row),
                pl.BlockSpec((d, tf), val_cols), pl.BlockSpec((d, tf), gate_cols),
                pl.BlockSpec((CONV_W, tf), val_cols), pl.BlockSpec((CONV_W, tf), gate_cols),
                pl.BlockSpec((1, tf), val_cols), pl.BlockSpec((1, tf), gate_cols),
                pl.BlockSpec((tf, d), lambda i, j: (jnp.maximum(j - 1, 0), 0)),
                pl.BlockSpec((1, d), const), pl.BlockSpec((1, d), const)]
    args = [h, x, w_up, w_up, conv_w, conv_w, conv_b.reshape(1, -1), conv_b.reshape(1, -1), w_down,
            g_post.reshape(1, d), g_next.reshape(1, d)]
    if has_state:
        in_specs += [pl.BlockSpec((n_seq, CONV_W - 1, tf), lambda i, j: (i, 0, up(j))),
                     pl.BlockSpec((n_seq, CONV_W - 1, tf), lambda i, j: (i, 0, nf + up(j)))]
        args += [state, state]
        scratch = []
    else:
        scratch = [pltpu.VMEM((nf, 8, tf), F32), pltpu.VMEM((nf, 8, tf), F32),
                   pltpu.VMEM((tm + 8, tf), F32), pltpu.VMEM((tm + 8, tf), F32)]
    conv_map = lambda i, j: (i, 0, up(j))
    out_specs = [pl.BlockSpec((tm, d), row)]
    out_shape = [jax.ShapeDtypeStruct((m, d), F32)]
    if emit_h:
        out_specs.append(pl.BlockSpec((tm, d), row))
        out_shape.append(jax.ShapeDtypeStruct((m, d), BF16))
    out_specs += [pl.BlockSpec((n_seq, CONV_W - 1, tf), conv_map)] * 2
    out_shape += [jax.ShapeDtypeStruct((n_seqs * tiles_per_seq, CONV_W - 1, d_ff), F32)] * 2
    res = pl.pallas_call(
        functools.partial(_ffn_kernel, n_seq=n_seq, has_state=has_state, emit_h=emit_h, tiles_per_seq=tiles_per_seq,
                          nf=nf),
        grid=(m // tm, nf + 1),
        in_specs=in_specs,
        out_specs=out_specs,
        out_shape=out_shape,
        scratch_shapes=[pltpu.VMEM((tm, d), F32), pltpu.VMEM((tm, tf), BF16), pltpu.VMEM((tm, tf), BF16)] + scratch,
        compiler_params=_cparams("arbitrary", "arbitrary"),
        name="conv_ffn_state" if has_state else "conv_ffn",
    )(*args)
    x_new = res[0]
    h_new = res[1] if emit_h else None
    conv_state = jnp.concatenate(res[-2:], axis=-1)[tiles_per_seq - 1::tiles_per_seq]
    return x_new, h_new, conv_state


def _ret_kernel(*refs, has_s0):
    if has_s0:
        lg_ref, q_ref, k_ref, v_ref, g_ref, s0_ref, y_ref, so_ref, s_ref = refs
    else:
        lg_ref, q_ref, k_ref, v_ref, g_ref, y_ref, so_ref, s_ref = refs
    c = pl.program_id(2)
    cl = q_ref.shape[0]

    @pl.when(c == 0)
    def _():
        s_ref[...] = s0_ref[0, 0] if has_s0 else jnp.zeros_like(s_ref)

    lg = lg_ref[0, 0:1, 0:1]
    q, k, v = q_ref[...], k_ref[...], v_ref[...]
    row = lax.broadcasted_iota(jnp.int32, (cl, cl), 0)
    col = lax.broadcasted_iota(jnp.int32, (cl, cl), 1)
    diff = (row - col).astype(F32)
    decay = jnp.where(diff >= 0, jnp.exp(jnp.maximum(diff, 0.0) * lg), 0.0)
    scores = _dot_nt(q.astype(BF16), k.astype(BF16)) * decay
    o = _dot(scores.astype(BF16), v)
    idx = lax.broadcasted_iota(jnp.int32, (cl, 1), 0).astype(F32)
    s_prev = s_ref[...]
    o = o + _dot((q * jnp.exp((idx + 1.0) * lg)).astype(BF16), s_prev.astype(BF16))
    k_dec = k * jnp.exp((cl - 1.0 - idx) * lg)
    s_new = jnp.exp(cl * lg) * s_prev + _dot(k_dec.T.astype(BF16), v)
    s_ref[...] = s_new
    mu = jnp.mean(o, axis=-1, keepdims=True)
    var = jnp.mean(jnp.square(o - mu), axis=-1, keepdims=True)
    of = (o - mu) * lax.rsqrt(var + GN_EPS)
    g = g_ref[...]
    y_ref[...] = (g * jax.nn.sigmoid(g) * of).astype(y_ref.dtype)

    @pl.when(c == pl.num_programs(2) - 1)
    def _():
        so_ref[0, 0] = s_new


def _retention(qk, v, g, batch, heads, s0=None, chunk=256):
    m = qk.shape[0]
    dk, dv = qk.shape[1] // (2 * heads), v.shape[1] // heads
    seq = m // batch
    cl = _tile(seq, chunk)
    nc = seq // cl
    lg = jnp.log1p(-jnp.exp2(-5.0 - jnp.arange(heads, dtype=F32)))
    lg_tab = jnp.broadcast_to(lg[:, None, None], (heads, 8, LANES))
    tok = lambda b, h, c: (b * nc + c, h)
    in_specs = [pl.BlockSpec((1, 8, LANES), lambda b, h, c: (h, 0, 0)),
                pl.BlockSpec((cl, dk), tok), pl.BlockSpec((cl, dk), lambda b, h, c: (b * nc + c, heads + h)),
                pl.BlockSpec((cl, dv), tok), pl.BlockSpec((cl, dv), tok)]
    args = [lg_tab, qk, qk, v, g]
    if s0 is not None:
        in_specs.append(pl.BlockSpec((1, 1, dk, dv), lambda b, h, c: (b, h, 0, 0)))
        args.append(s0)
    return pl.pallas_call(
        functools.partial(_ret_kernel, has_s0=s0 is not None),
        grid=(batch, heads, nc),
        in_specs=in_specs,
        out_specs=[pl.BlockSpec((cl, dv), tok), pl.BlockSpec((1, 1, dk, dv), lambda b, h, c: (b, h, 0, 0))],
        out_shape=[jax.ShapeDtypeStruct((m, heads * dv), BF16), jax.ShapeDtypeStruct((batch, heads, dk, dv), F32)],
        scratch_shapes=[pltpu.VMEM((dk, dv), F32)],
        compiler_params=_cparams("parallel", "parallel", "arbitrary"),
        name="retention",
    )(*args)


def _mla_down_kernel(h_ref, wq_ref, wkv_ref, wr_ref, wrr_ref, gq_ref, gkv_ref, cos_ref, sin_ref,
                     cq_ref, ckv_ref, kr_ref):
    h = h_ref[...]
    cq_ref[...] = _rms(_dot(h, wq_ref[...]), gq_ref[...]).astype(cq_ref.dtype)
    ckv_ref[...] = _rms(_dot(h, wkv_ref[...]), gkv_ref[...])
    kr_ref[...] = _dot(h, wr_ref[...]) * cos_ref[...] + _dot(h, wrr_ref[...]) * sin_ref[...]


def _mla_down(h, wq, wkv, wr, wrr, gq, gkv, cos, sin, tm=512):
    m, d = h.shape
    ql, kvl, rd = wq.shape[1], wkv.shape[1], wr.shape[1]
    ltab = cos.shape[0]
    tm = _tile(math.gcd(m, ltab), tm)
    ntab = ltab // tm
    const = lambda i: (0, 0)
    row = lambda i: (i, 0)
    tab = lambda i: (i % ntab, 0)
    return pl.pallas_call(
        _mla_down_kernel,
        grid=(m // tm,),
        in_specs=[pl.BlockSpec((tm, d), row), pl.BlockSpec((d, ql), const), pl.BlockSpec((d, kvl), const),
                  pl.BlockSpec((d, rd), const), pl.BlockSpec((d, rd), const),
                  pl.BlockSpec((1, ql), const), pl.BlockSpec((1, kvl), const),
                  pl.BlockSpec((tm, rd), tab), pl.BlockSpec((tm, rd), tab)],
        out_specs=[pl.BlockSpec((tm, ql), row), pl.BlockSpec((tm, kvl), row), pl.BlockSpec((tm, rd), row)],
        out_shape=[jax.ShapeDtypeStruct((m, ql), BF16), jax.ShapeDtypeStruct((m, kvl), F32),
                   jax.ShapeDtypeStruct((m, rd), F32)],
        compiler_params=_cparams("parallel"),
        name="mla_down",
    )(h, wq, wkv, wr, wrr, gq.reshape(1, ql), gkv.reshape(1, kvl), cos, sin)


def _mla_q_kernel(cq_ref, wn_ref, wr_ref, wrr_ref, cos_ref, sin_ref, q_ref, *, scale):
    cq = cq_ref[...]
    qn = _dot(cq, wn_ref[...]) * scale
    nope = LANES
    for hh in range(q_ref.shape[0]):
        q_ref[hh, :, :nope] = qn[:, hh * LANES:(hh + 1) * LANES].astype(q_ref.dtype)
    a, b = _dot(cq, wr_ref[...]), _dot(cq, wrr_ref[...])
    cos, sin = cos_ref[...] * scale, sin_ref[...] * scale
    rd = q_ref.shape[2] - nope
    per = LANES // rd
    for p in range(a.shape[1] // LANES):
        r = a[:, p * LANES:(p + 1) * LANES] * cos + b[:, p * LANES:(p + 1) * LANES] * sin
        for s in range(per):
            q_ref[p * per + s, :, nope:] = r[:, s * rd:(s + 1) * rd].astype(q_ref.dtype)


def _mla_q(cq, wn, wr, wrr, cos, sin, heads, scale, tm=512):
    m, ql = cq.shape
    nope, rd = wn.shape[1] // heads, wr.shape[1] // heads
    assert nope == LANES and LANES % rd == 0
    ltab = cos.shape[0]
    tm = _tile(math.gcd(m, ltab), tm)
    ntab = ltab // tm
    const = lambda i: (0, 0)
    tab = lambda i: (i % ntab, 0)
    return pl.pallas_call(
        functools.partial(_mla_q_kernel, scale=scale),
        grid=(m // tm,),
        in_specs=[pl.BlockSpec((tm, ql), lambda i: (i, 0)), pl.BlockSpec(wn.shape, const),
                  pl.BlockSpec(wr.shape, const), pl.BlockSpec(wrr.shape, const),
                  pl.BlockSpec((tm, LANES), tab), pl.BlockSpec((tm, LANES), tab)],
        out_specs=pl.BlockSpec((heads, tm, nope + rd), lambda i: (0, i, 0)),
        out_shape=jax.ShapeDtypeStruct((heads, m, nope + rd), BF16),
        compiler_params=_cparams("parallel"),
        name="mla_q",
    )(cq, wn, wr, wrr, cos, sin)


def _mla_kv_kernel(ckv_ref, kr_ref, wk_ref, wv_ref, k_ref, v_ref):
    ckv = ckv_ref[...].astype(BF16)
    kn, v = _dot(ckv, wk_ref[...]), _dot(ckv, wv_ref[...])
    kr = kr_ref[...].astype(k_ref.dtype)
    nope = k_ref.shape[2] - kr.shape[1]
    vd = v_ref.shape[2]
    for hh in range(k_ref.shape[0]):
        k_ref[hh, :, :nope] = kn[:, hh * nope:(hh + 1) * nope].astype(k_ref.dtype)
        k_ref[hh, :, nope:] = kr
        v_ref[hh] = v[:, hh * vd:(hh + 1) * vd].astype(v_ref.dtype)


def _mla_kv(ckv, kr, wk, wv, heads, tm=512):
    m, kvl = ckv.shape
    rd = kr.shape[1]
    nope, vd = wk.shape[1] // heads, wv.shape[1] // heads
    tm = _tile(m, tm)
    const = lambda i: (0, 0)
    return pl.pallas_call(
        _mla_kv_kernel,
        grid=(m // tm,),
        in_specs=[pl.BlockSpec((tm, kvl), lambda i: (i, 0)), pl.BlockSpec((tm, rd), lambda i: (i, 0)),
                  pl.BlockSpec(wk.shape, const), pl.BlockSpec(wv.shape, const)],
        out_specs=[pl.BlockSpec((heads, tm, nope + rd), lambda i: (0, i, 0)),
                   pl.BlockSpec((heads, tm, vd), lambda i: (0, i, 0))],
        out_shape=[jax.ShapeDtypeStruct((heads, m, nope + rd), BF16), jax.ShapeDtypeStruct((heads, m, vd), BF16)],
        compiler_params=_cparams("parallel"),
        name="mla_kv",
    )(ckv, kr, wk, wv)


def _mla_flash_kernel(q_ref, k_ref, v_ref, o_ref, m_ref, l_ref, acc_ref):
    qi, ki = pl.program_id(1), pl.program_id(2)
    nh, tq, vd = acc_ref.shape
    tk = k_ref.shape[1]

    @pl.when(ki == 0)
    def _():
        m_ref[...] = jnp.full_like(m_ref, -jnp.inf)
        l_ref[...] = jnp.zeros_like(l_ref)
        acc_ref[...] = jnp.zeros_like(acc_ref)

    def sweep(diagonal):
        if diagonal:
            row_chunk = lax.broadcasted_iota(jnp.int32, (tq, tk), 0) // CHUNK
            col_chunk = lax.broadcasted_iota(jnp.int32, (tq, tk), 1) // CHUNK
            mask = col_chunk <= row_chunk

        def head(hh, carry):
            s = _dot_nt(q_ref[hh], k_ref[hh])
            if diagonal:
                s = jnp.where(mask, s, -jnp.inf)
            m_prev = m_ref[hh]
            m_new = jnp.maximum(m_prev, jnp.max(s, axis=-1, keepdims=True))
            alpha = jnp.exp(m_prev - m_new)
            p = jnp.concatenate([jnp.exp(s[:, c * LANES:(c + 1) * LANES] - m_new) for c in range(tk // LANES)],
                                axis=-1)
            l_ref[hh] = alpha * l_ref[hh] + jnp.sum(p, axis=-1, keepdims=True)
            acc_ref[hh] = alpha * acc_ref[hh] + _dot(p.astype(BF16), v_ref[hh])
            m_ref[hh] = m_new
            return carry

        lax.fori_loop(0, nh, head, 0, unroll=4)

    @pl.when(ki < qi)
    def _():
        sweep(False)

    @pl.when(ki == qi)
    def _():
        sweep(True)
        for hh in range(nh):
            o_ref[:, hh * vd:(hh + 1) * vd] = (acc_ref[hh] / l_ref[hh]).astype(o_ref.dtype)


def _mla_flash(q, k, v, batch, t=512):
    heads, m, qd = q.shape
    vd = v.shape[2]
    seq = m // batch
    t = _tile(seq, t)
    assert t % CHUNK == 0 and vd == LANES
    nq = seq // t
    qmap = lambda b, qi, ki: (0, b * nq + qi, 0)
    kmap = lambda b, qi, ki: (0, b * nq + jnp.minimum(ki, qi), 0)
    return pl.pallas_call(
        _mla_flash_kernel,
        grid=(batch, nq, nq),
        in_specs=[pl.BlockSpec((heads, t, qd), qmap), pl.BlockSpec((heads, t, qd), kmap),
                  pl.BlockSpec((heads, t, vd), kmap)],
        out_specs=pl.BlockSpec((t, heads * vd), lambda b, qi, ki: (b * nq + qi, 0)),
        out_shape=jax.ShapeDtypeStruct((m, heads * vd), BF16),
        scratch_shapes=[pltpu.VMEM((heads, t, LANES), F32), pltpu.VMEM((heads, t, LANES), F32),
                        pltpu.VMEM((heads, t, vd), F32)],
        compiler_params=_cparams("parallel", "parallel", "arbitrary"),
        name="mla_flash",
    )(q, k, v)


def _mla_sample_kernel(q_ref, ckv_ref, kr_ref, cckv_ref, ckr_ref, wuk_ref, wuv_ref, o_ref, ql_ref):
    nh, t, qd = q_ref.shape
    kvl = ckv_ref.shape[1]
    nope, vd = wuk_ref.shape[2], wuv_ref.shape[2]
    for hh in range(nh):
        ql_ref[hh] = _dot_nt(q_ref[hh, :, :nope], wuk_ref[hh]).astype(BF16)
    q_lat = ql_ref[...].reshape(nh * t, kvl)
    q_rope = q_ref[:, :, nope:].reshape(nh * t, qd - nope)
    c_old, r_old = cckv_ref[0].astype(BF16), ckr_ref[0].astype(BF16)
    c_new, r_new = ckv_ref[...].astype(BF16), kr_ref[...].astype(BF16)
    s_old = _dot_nt(q_lat, c_old) + _dot_nt(q_rope, r_old)
    s_new = _dot_nt(q_lat, c_new) + _dot_nt(q_rope, r_new)
    mx = jnp.maximum(jnp.max(s_old, axis=-1, keepdims=True), jnp.max(s_new, axis=-1, keepdims=True))
    p_old, p_new = jnp.exp(s_old - mx), jnp.exp(s_new - mx)
    denom = jnp.sum(p_old, axis=-1, keepdims=True) + jnp.sum(p_new, axis=-1, keepdims=True)
    o_lat = (_dot(p_old.astype(BF16), c_old) + _dot(p_new.astype(BF16), c_new)) / denom
    o_lat = o_lat.astype(BF16)
    for hh in range(nh):
        o_ref[:, hh * vd:(hh + 1) * vd] = _dot(o_lat[hh * t:(hh + 1) * t], wuv_ref[hh]).astype(o_ref.dtype)


def _mla_sample(q, ckv, kr, cache_ckv, cache_kr, wuk_t, wuv_t):
    heads, m, qd = q.shape
    batch, past, kvl = cache_ckv.shape
    t = m // batch
    rd = kr.shape[1]
    vd = wuv_t.shape[2]
    const3 = lambda b: (0, 0, 0)
    return pl.pallas_call(
        _mla_sample_kernel,
        grid=(batch,),
        in_specs=[pl.BlockSpec((heads, t, qd), lambda b: (0, b, 0)),
                  pl.BlockSpec((t, kvl), lambda b: (b, 0)), pl.BlockSpec((t, rd), lambda b: (b, 0)),
                  pl.BlockSpec((1, past, kvl), lambda b: (b, 0, 0)), pl.BlockSpec((1, past, rd), lambda b: (b, 0, 0)),
                  pl.BlockSpec(wuk_t.shape, const3), pl.BlockSpec(wuv_t.shape, const3)],
        out_specs=pl.BlockSpec((t, heads * vd), lambda b: (b, 0)),
        out_shape=jax.ShapeDtypeStruct((m, heads * vd), BF16),
        scratch_shapes=[pltpu.VMEM((heads, t, kvl), BF16)],
        compiler_params=_cparams("parallel"),
        name="mla_sample",
    )(q, ckv, kr, cache_ckv, cache_kr, wuk_t, wuv_t)


def _s5_kernel(*refs, has_h0, col_block):
    (x_ref, g_ref, wbr_ref, wbi_ref, ar_ref, ai_ref, cr_ref, ci_ref, d_ref), refs = refs[:9], refs[9:]
    if has_h0:
        (h0r_ref, h0i_ref), refs = refs[:2], refs[2:]
    y_ref, sr_ref, si_ref, br_ref, bi_ref, hr_ref, hi_ref = refs
    tt = pl.program_id(1)
    t, d = x_ref.shape
    n_state = br_ref.shape[1]
    n_gb, gin, gst = wbr_ref.shape

    @pl.when(tt == 0)
    def _():
        if has_h0:
            hr_ref[...] = h0r_ref[0]
            hi_ref[...] = h0i_ref[0]
        else:
            hr_ref[...] = jnp.zeros_like(hr_ref)
            hi_ref[...] = jnp.zeros_like(hi_ref)

    h = _rms(x_ref[...], g_ref[...])
    ub = h.astype(BF16)
    for gb in range(n_gb):
        u_blk = ub[:, gb * gin:(gb + 1) * gin]
        br_ref[:, gb * gst:(gb + 1) * gst] = _dot(u_blk, wbr_ref[gb])
        bi_ref[:, gb * gst:(gb + 1) * gst] = _dot(u_blk, wbi_ref[gb])

    for cb in range(n_state // col_block):
        sl = slice(cb * col_block, (cb + 1) * col_block)
        a_re, a_im = ar_ref[:, sl], ai_ref[:, sl]

        def step(s, carry):
            h_re, h_im = carry
            n_re = a_re * h_re - a_im * h_im + br_ref[pl.ds(s, 1), sl]
            n_im = a_re * h_im + a_im * h_re + bi_ref[pl.ds(s, 1), sl]
            br_ref[pl.ds(s, 1), sl] = n_re
            bi_ref[pl.ds(s, 1), sl] = n_im
            return n_re, n_im

        h_re, h_im = lax.fori_loop(0, t, step, (hr_ref[:, sl], hi_ref[:, sl]))
        hr_ref[:, sl] = h_re
        hi_ref[:, sl] = h_im

    for gb in range(n_gb):
        hs_re = br_ref[:, gb * gst:(gb + 1) * gst].astype(BF16)
        hs_im = bi_ref[:, gb * gst:(gb + 1) * gst].astype(BF16)
        cs = slice(gb * gin, (gb + 1) * gin)
        y = _dot(hs_re, cr_ref[gb]) - _dot(hs_im, ci_ref[gb]) + d_ref[:, cs] * h[:, cs]
        y_ref[:, cs] = _gelu_tanh(y).astype(y_ref.dtype)

    @pl.when(tt == pl.num_programs(1) - 1)
    def _():
        sr_ref[0] = hr_ref[...]
        si_ref[0] = hi_ref[...]


def _s5(x, g, wb_re, wb_im, a_re, a_im, wc_re, wc_im, d_skip, batch, h0=None, t=256, col_block=4096):
    m, d = x.shape
    seq = m // batch
    t = _tile(seq, t)
    nt = seq // t
    n_state = a_re.shape[1]
    col_block = _tile(n_state, col_block)
    const2 = lambda b, tt: (0, 0)
    const3 = lambda b, tt: (0, 0, 0)
    in_specs = [pl.BlockSpec((t, d), lambda b, tt: (b * nt + tt, 0)), pl.BlockSpec((1, d), const2),
                pl.BlockSpec(wb_re.shape, const3), pl.BlockSpec(wb_im.shape, const3),
                pl.BlockSpec((1, n_state), const2), pl.BlockSpec((1, n_state), const2),
                pl.BlockSpec(wc_re.shape, const3), pl.BlockSpec(wc_im.shape, const3),
                pl.BlockSpec((1, d), const2)]
    args = [x, g.reshape(1, d), wb_re, wb_im, a_re, a_im, wc_re, wc_im, d_skip.reshape(1, d)]
    if h0 is not None:
        in_specs += [pl.BlockSpec((1, 1, n_state), lambda b, tt: (b, 0, 0))] * 2
        args += [h0[0].reshape(batch, 1, n_state), h0[1].reshape(batch, 1, n_state)]
    state_spec = pl.BlockSpec((1, 1, n_state), lambda b, tt: (b, 0, 0))
    return pl.pallas_call(
        functools.partial(_s5_kernel, has_h0=h0 is not None, col_block=col_block),
        grid=(batch, nt),
        in_specs=in_specs,
        out_specs=[pl.BlockSpec((t, d), lambda b, tt: (b * nt + tt, 0)), state_spec, state_spec],
        out_shape=[jax.ShapeDtypeStruct((m, d), BF16), jax.ShapeDtypeStruct((batch, 1, n_state), F32),
                   jax.ShapeDtypeStruct((batch, 1, n_state), F32)],
        scratch_shapes=[pltpu.VMEM((t, n_state), F32), pltpu.VMEM((t, n_state), F32),
                        pltpu.VMEM((1, n_state), F32), pltpu.VMEM((1, n_state), F32)],
        compiler_params=_cparams("parallel", "arbitrary"),
        name="s5_scan",
    )(*args)


def _s5_params(a_re, a_im, log_dt, b_re, b_im, c_re, c_im, groups_per_block=8):
    g, p, gc = b_re.shape
    step = jnp.exp(log_dt)[:, None]
    xr, xi = a_re * step, a_im * step
    ab_re, ab_im = jnp.exp(xr) * jnp.cos(xi), jnp.exp(xr) * jnp.sin(xi)
    m1_re = jnp.expm1(xr) * jnp.cos(xi) - 2.0 * jnp.square(jnp.sin(0.5 * xi))
    den = a_re * a_re + a_im * a_im
    f_re = (m1_re * a_re + ab_im * a_im) / den
    f_im = (ab_im * a_re - m1_re * a_im) / den
    bb_re = f_re[..., None] * b_re - f_im[..., None] * b_im
    bb_im = f_re[..., None] * b_im + f_im[..., None] * b_re
    gpb = groups_per_block
    eye = jnp.eye(gpb, dtype=F32)

    def blockdiag_b(w):
        w = w.reshape(g // gpb, gpb, p, gc)
        return jnp.einsum('bgpc,gh->bgchp', w, eye).reshape(g // gpb, gpb * gc, gpb * p).astype(BF16)

    def blockdiag_c(w):
        w = w.reshape(g // gpb, gpb, gc, p)
        return jnp.einsum('bgcp,gh->bgphc', w, eye).reshape(g // gpb, gpb * p, gpb * gc).astype(BF16)

    return (blockdiag_b(bb_re), blockdiag_b(bb_im), ab_re.reshape(1, g * p), ab_im.reshape(1, g * p),
            blockdiag_c(c_re), blockdiag_c(c_im))


def _sb_block(q, k, v, mask, tri, r_prev, scale):
    z = _dot_nt(q, k) * scale
    sp = _softplus(z)
    if mask is not None:
        sp = jnp.where(mask, sp, 0.0)
    hi = sp.astype(BF16)
    lo = (sp - hi.astype(F32)).astype(BF16)
    csum = _dot(hi, tri) + _dot(lo, tri)
    log_w = z - csum
    if mask is not None:
        log_w = jnp.where(mask, log_w, -jnp.inf)
    n_keys = z.shape[1]
    if n_keys % LANES:
        w = jnp.exp(log_w - r_prev[:, :n_keys])
    else:
        w = jnp.concatenate([jnp.exp(log_w[:, c * LANES:(c + 1) * LANES] - r_prev) for c in range(n_keys // LANES)],
                            axis=-1)
    return _dot(w.astype(BF16), v), r_prev + jnp.sum(sp, axis=-1, keepdims=True)


def _tri(n):
    row = lax.broadcasted_iota(jnp.int32, (n, n), 0)
    col = lax.broadcasted_iota(jnp.int32, (n, n), 1)
    return (row >= col).astype(BF16)


def _sb_flash_kernel(q_ref, k_hbm, v_hbm, o_ref, kbuf, vbuf, sem, r_ref, acc_ref, *, scale, seq):
    b, qi = pl.program_id(0), pl.program_id(1)
    nh, t, hd = acc_ref.shape
    row0 = b * seq + qi * t

    def copies(kj, slot):
        start = pl.multiple_of(row0 - kj * t, t)
        return (pltpu.make_async_copy(k_hbm.at[:, pl.ds(start, t), :], kbuf.at[slot], sem.at[0, slot]),
                pltpu.make_async_copy(v_hbm.at[:, pl.ds(start, t), :], vbuf.at[slot], sem.at[1, slot]))

    for cp in copies(0, 0):
        cp.start()
    r_ref[...] = jnp.zeros_like(r_ref)
    acc_ref[...] = jnp.zeros_like(acc_ref)
    row = lax.broadcasted_iota(jnp.int32, (t, t), 0)
    col = lax.broadcasted_iota(jnp.int32, (t, t), 1)
    tri = _tri(t)

    def sweep(carry):
        kj, _ = carry
        slot = kj % 2
        for cp in copies(kj, slot):
            cp.wait()

        @pl.when(kj < qi)
        def _():
            for cp in copies(kj + 1, 1 - slot):
                cp.start()

        mask = (col < row) | (kj > 0)

        def head(hh, r_min):
            o, r = _sb_block(q_ref[hh], kbuf[slot, hh], vbuf[slot, hh], mask, tri, r_ref[hh], scale)
            acc_ref[hh] += o
            r_ref[hh] = r
            return jnp.minimum(r_min, jnp.min(r))

        r_min = lax.fori_loop(0, nh, head, jnp.float32(jnp.inf), unroll=4)
        return kj + 1, (r_min > SB_EXIT_SUM).astype(jnp.int32)

    kj_end, _ = lax.while_loop(lambda c: (c[0] <= qi) & (c[1] == 0), sweep, (jnp.int32(0), jnp.int32(0)))

    @pl.when(kj_end <= qi)
    def _():
        for cp in copies(kj_end, kj_end % 2):
            cp.wait()

    for hh in range(nh):
        o_ref[:, hh * hd:(hh + 1) * hd] = acc_ref[hh].astype(o_ref.dtype)


def _sb_flash(q, k, v, batch, scale, t=256):
    heads, m, hd = q.shape
    seq = m // batch
    t = _tile(seq, t)
    nq = seq // t
    return pl.pallas_call(
        functools.partial(_sb_flash_kernel, scale=scale, seq=seq),
        grid=(batch, nq),
        in_specs=[pl.BlockSpec((heads, t, hd), lambda b, qi: (0, b * nq + qi, 0)),
                  pl.BlockSpec(memory_space=pl.ANY), pl.BlockSpec(memory_space=pl.ANY)],
        out_specs=pl.BlockSpec((t, heads * hd), lambda b, qi: (b * nq + qi, 0)),
        out_shape=jax.ShapeDtypeStruct((m, heads * hd), BF16),
        scratch_shapes=[pltpu.VMEM((2, heads, t, hd), BF16), pltpu.VMEM((2, heads, t, hd), BF16),
                        pltpu.SemaphoreType.DMA((2, 2)),
                        pltpu.VMEM((heads, t, LANES), F32), pltpu.VMEM((heads, t, hd), F32)],
        compiler_params=_cparams("parallel", "parallel"),
        name="sb_flash",
    )(q, k, v)


def _sb_sample_kernel(q_ref, kn_ref, vn_ref, kc_hbm, vc_hbm, o_ref, kbuf, vbuf, sem, r_ref, acc_ref, *,
                      scale, layer, blk):
    b = pl.program_id(0)
    nh, t, hd = acc_ref.shape
    nblk = kc_hbm.shape[2] // blk

    def copies(kb, slot):
        start = pl.multiple_of((nblk - 1 - kb) * blk, blk)
        cps = []
        for hh in range(nh):
            cps.append(pltpu.make_async_copy(kc_hbm.at[layer, b, pl.ds(start, blk), hh], kbuf.at[slot, hh],
                                             sem.at[0, slot]))
            cps.append(pltpu.make_async_copy(vc_hbm.at[layer, b, pl.ds(start, blk), hh], vbuf.at[slot, hh],
                                             sem.at[1, slot]))
        return cps

    for cp in copies(0, 0):
        cp.start()
    row = lax.broadcasted_iota(jnp.int32, (t, t), 0)
    col = lax.broadcasted_iota(jnp.int32, (t, t), 1)
    own_mask, own_tri = col < row, _tri(t)

    def own(hh, carry):
        o, r = _sb_block(q_ref[hh], kn_ref[hh], vn_ref[hh], own_mask, own_tri, jnp.zeros((t, LANES), F32), scale)
        acc_ref[hh] = o
        r_ref[hh] = r
        return carry

    lax.fori_loop(0, nh, own, 0)
    tri = _tri(blk)

    def sweep(carry):
        kb, _ = carry
        slot = kb % 2
        for cp in copies(kb, slot):
            cp.wait()

        @pl.when(kb + 1 < nblk)
        def _():
            for cp in copies(kb + 1, 1 - slot):
                cp.start()

        def head(hh, r_min):
            o, r = _sb_block(q_ref[hh], kbuf[slot, hh].astype(BF16), vbuf[slot, hh].astype(BF16), None, tri,
                             r_ref[hh], scale)
            acc_ref[hh] += o
            r_ref[hh] = r
            return jnp.minimum(r_min, jnp.min(r))

        r_min = lax.fori_loop(0, nh, head, jnp.float32(jnp.inf))
        return kb + 1, (r_min > SB_EXIT_SUM).astype(jnp.int32)

    kb_end, _ = lax.while_loop(lambda c: (c[0] < nblk) & (c[1] == 0), sweep, (jnp.int32(0), jnp.int32(0)))

    @pl.when(kb_end < nblk)
    def _():
        for cp in copies(kb_end, kb_end % 2):
            cp.wait()

    for hh in range(nh):
        o_ref[:, hh * hd:(hh + 1) * hd] = acc_ref[hh].astype(o_ref.dtype)


def _sb_sample(q, k_new, v_new, cache_k, cache_v, layer, scale, blk=256):
    heads, m, hd = q.shape
    batch, past = cache_k.shape[1], cache_k.shape[2]
    t = m // batch
    blk = _tile(past, blk)
    new = lambda b: (0, b, 0)
    return pl.pallas_call(
        functools.partial(_sb_sample_kernel, scale=scale, layer=layer, blk=blk),
        grid=(batch,),
        in_specs=[pl.BlockSpec((heads, t, hd), new), pl.BlockSpec((heads, t, hd), new),
                  pl.BlockSpec((heads, t, hd), new),
                  pl.BlockSpec(memory_space=pl.ANY), pl.BlockSpec(memory_space=pl.ANY)],
        out_specs=pl.BlockSpec((t, heads * hd), lambda b: (b, 0)),
        out_shape=jax.ShapeDtypeStruct((m, heads * hd), BF16),
        scratch_shapes=[pltpu.VMEM((2, heads, blk, hd), F32), pltpu.VMEM((2, heads, blk, hd), F32),
                        pltpu.SemaphoreType.DMA((2, 2)),
                        pltpu.VMEM((heads, t, LANES), F32), pltpu.VMEM((heads, t, hd), F32)],
        compiler_params=_cparams("parallel"),
        name="sb_sample",
    )(q, k_new, v_new, cache_k, cache_v)


def _rope_tables(pos, d, reps):
    half = d // 2
    inv = ROPE_BASE ** (-jnp.arange(half, dtype=F32) * 2.0 / d)
    ang = pos.astype(F32)[:, None] * inv[None, :]
    return jnp.tile(jnp.cos(ang), (1, reps)), jnp.tile(jnp.sin(ang), (1, reps))


def _rot_half_cols(w, d):
    k, n = w.shape
    w = w.reshape(k, n // d, d)
    half = d // 2
    return jnp.concatenate([-w[..., half:], w[..., :half]], axis=-1).reshape(k, n)


def kernel(x_prompt, x_sample, state_ret, cache_mla_ckv, cache_mla_krope, state_s5_re, state_s5_im, cache_sb_k, cache_sb_v, state_ffn_conv, norm_gains, ret_w_in, ret_w_out, mla_w_down, mla_g_q, mla_g_kv, mla_w_uq, mla_w_uk, mla_w_uv, mla_w_o, s5_a_re, s5_a_im, s5_log_dt, s5_b_re, s5_b_im, s5_c_re, s5_c_im, s5_d, s5_w_glu, s5_b_glu, sb_w_qkv, sb_w_o, ffn_w_up, ffn_conv_w, ffn_conv_b, ffn_w_down):
    n_p, seq_p, d = x_prompt.shape
    n_s, seq_s, _ = x_sample.shape
    depth = norm_gains.shape[0]
    n_mixers = 4
    streams = (dict(batch=n_p, seq=seq_p, pos=jnp.arange(seq_p)),
               dict(batch=n_s, seq=seq_s, pos=None))
    xs = [x_prompt.reshape(n_p * seq_p, d), x_sample.reshape(n_s * seq_s, d)]
    hs = [_rmsnorm(x, norm_gains[0, 0]) for x in xs]
    outs = {name: ([], []) for name in ('ret', 'ckv', 'kr', 's5re', 's5im', 'sbk', 'sbv', 'conv')}

    past_len = cache_mla_ckv.shape[2]

    def positions(si):
        if si == 0:
            return jnp.arange(seq_p)
        return jnp.tile(past_len + jnp.arange(seq_s), n_s)

    for i in range(depth):
        j, kind = i // n_mixers, i % n_mixers
        g = norm_gains[i]
        g_next = norm_gains[i + 1, 0] if i + 1 < depth else g[0]
        emit_next = i + 1 < depth and (i + 1) % n_mixers != 2
        ffn_up = ffn_w_up[i].astype(BF16)
        ffn_down = ffn_w_down[i].astype(BF16)
        for si, st in enumerate(streams):
            x, h, batch = xs[si], hs[si], st['batch']
            if kind == 0:
                heads, dk, dv = state_ret.shape[2], state_ret.shape[3], state_ret.shape[4]
                qk_w, v_w = heads * dk, heads * dv
                w_in = ret_w_in[j].astype(BF16)
                cos, sin = _rope_tables(positions(si), dk, 1)
                qk = _linear_rope(h, w_in, 2 * qk_w, cos, sin, dk, qk_w, dk ** -0.5)
                v, = _linear(h, w_in, 2 * qk_w, v_w, (BF16,), name="ret_v")
                gate, = _linear(h, w_in, 2 * qk_w + v_w, v_w, (F32,), name="ret_g")
                s0 = state_ret[j] if si == 1 else None
                y, s1 = _retention(qk, v, gate, batch, heads, s0)
                outs['ret'][si].append(s1)
                x, h = _outproj(y, ret_w_out[j].astype(BF16), x, g[1], g[2], name="ret_outproj")
            elif kind == 1:
                ql, kvl = mla_g_q.shape[1], mla_g_kv.shape[1]
                heads, nope = mla_w_uk.shape[2], mla_w_uk.shape[3]
                rd, vd = cache_mla_krope.shape[3], mla_w_uv.shape[3]
                past = cache_mla_ckv.shape[2]
                scale = (nope + rd) ** -0.5
                w_down = mla_w_down[j]
                w_kr = w_down[:, ql + kvl:]
                pos = positions(si)
                cos_k, sin_k = _rope_tables(pos, rd, 2)
                cos_q, sin_q = _rope_tables(pos, rd, 2 * LANES // rd)
                cq, ckv, kr = _mla_down(h, w_down[:, :ql].astype(BF16), w_down[:, ql:ql + kvl].astype(BF16),
                                        w_kr.astype(BF16), _rot_half_cols(w_kr, rd).astype(BF16),
                                        mla_g_q[j], mla_g_kv[j], cos_k, sin_k)
                w_uq = mla_w_uq[j].reshape(ql, heads, nope + rd)
                w_qn = w_uq[:, :, :nope].reshape(ql, heads * nope)
                w_qr = w_uq[:, :, nope:].reshape(ql, heads * rd)
                q = _mla_q(cq, w_qn.astype(BF16), w_qr.astype(BF16), _rot_half_cols(w_qr, rd).astype(BF16),
                           cos_q, sin_q, heads, scale)
                outs['ckv'][si].append(ckv.reshape(batch, st['seq'], kvl))
                outs['kr'][si].append(kr.reshape(batch, st['seq'], rd))
                if si == 0:
                    k, v = _mla_kv(ckv, kr, mla_w_uk[j].reshape(kvl, heads * nope).astype(BF16),
                                   mla_w_uv[j].reshape(kvl, heads * vd).astype(BF16), heads)
                    o = _mla_flash(q, k, v, batch)
                else:
                    o = _mla_sample(q, ckv, kr, cache_mla_ckv[j], cache_mla_krope[j],
                                    jnp.transpose(mla_w_uk[j], (1, 0, 2)).astype(BF16),
                                    jnp.transpose(mla_w_uv[j], (1, 0, 2)).astype(BF16))
                x, h = _outproj(o, mla_w_o[j].astype(BF16), x, g[1], g[2], name="mla_outproj")
            elif kind == 2:
                params = _s5_params(s5_a_re[j], s5_a_im[j], s5_log_dt[j], s5_b_re[j], s5_b_im[j],
                                    s5_c_re[j], s5_c_im[j])
                h0 = (state_s5_re[j], state_s5_im[j]) if si == 1 else None
                y, s_re, s_im = _s5(x, g[0], *params, s5_d[j], batch, h0)
                outs['s5re'][si].append(s_re.reshape(batch, *s5_a_re.shape[1:]))
                outs['s5im'][si].append(s_im.reshape(batch, *s5_a_re.shape[1:]))
                x, h = _glu_outproj(y, s5_w_glu[j].astype(BF16), s5_b_glu[j], x, g[1], g[2])
            else:
                heads, hd = cache_sb_k.shape[3], cache_sb_k.shape[4]
                past = cache_sb_k.shape[2]
                w_qkv = sb_w_qkv[j].astype(BF16)
                scale = hd ** -0.5
                q, = _linear(h, w_qkv, 0, heads * hd, (BF16,), (True,), name="sb_q")
                k32, k16 = _linear(h, w_qkv, heads * hd, heads * hd, (F32, BF16), (False, True), name="sb_k")
                v32, v16 = _linear(h, w_qkv, 2 * heads * hd, heads * hd, (F32, BF16), (False, True), name="sb_v")
                outs['sbk'][si].append(k32.reshape(batch, st['seq'], heads, hd))
                outs['sbv'][si].append(v32.reshape(batch, st['seq'], heads, hd))
                if si == 0:
                    o = _sb_flash(q, k16, v16, batch, scale)
                else:
                    o = _sb_sample(q, k16, v16, cache_sb_k, cache_sb_v, j, scale)
                x, h = _outproj(o, sb_w_o[j].astype(BF16), x, g[1], g[2], name="sb_outproj")
            x, h, conv = _conv_ffn(h, x, ffn_up, ffn_conv_w[i], ffn_conv_b[i], ffn_down, g[3], g_next, st['seq'],
                                   state=state_ffn_conv[i] if si == 1 else None, emit_h=emit_next)
            outs['conv'][si].append(conv)
            xs[si], hs[si] = x, h

    stack = lambda name: (jnp.stack(outs[name][0]), jnp.stack(outs[name][1]))
    return (xs[0].reshape(x_prompt.shape), xs[1].reshape(x_sample.shape),
            *stack('ret'), *stack('ckv'), *stack('kr'), *stack('s5re'), *stack('s5im'),
            *stack('sbk'), *stack('sbv'), *stack('conv'))
```

```python
import functools
import math

import jax
import jax.numpy as jnp
from jax import lax
from jax.experimental import pallas as pl
from jax.experimental.pallas import tpu as pltpu

F32 = jnp.float32
BF16 = jnp.bfloat16
NORM_EPS = 1e-6
GN_EPS = 1e-5
ROPE_BASE = 10000.0
CHUNK = 64
CONV_W = 3
LANES = 128
VMEM_LIMIT = 56 * 1024 * 1024
SB_EXIT_SUM = 104.0
FFN_COLS = 512


def _cparams(*sem):
    return pltpu.CompilerParams(dimension_semantics=sem, vmem_limit_bytes=VMEM_LIMIT)


def _tile(n, pref):
    t = min(n, pref)
    while n % t:
        t -= 1
    return t


def _dot(a, b):
    return jnp.dot(a, b, preferred_element_type=F32)


def _dot_nt(a, b):
    return lax.dot_general(a, b, (((1,), (1,)), ((), ())), preferred_element_type=F32)


def _rms(x, g):
    return x * lax.rsqrt(jnp.mean(x * x, axis=-1, keepdims=True) + NORM_EPS) * g


def _gelu_tanh(x):
    k = -2.0 * math.sqrt(2.0 / math.pi) * math.log2(math.e)
    return x / (1.0 + jnp.exp2(x * (k + (k * 0.044715) * (x * x))))


def _softplus(z):
    return jnp.maximum(z, 0.0) + jnp.log(1.0 + jnp.exp(-jnp.abs(z)))


def _rmsnorm_kernel(x_ref, g_ref, o_ref):
    o_ref[...] = _rms(x_ref[...], g_ref[...]).astype(o_ref.dtype)


def _rmsnorm(x, g, out_dtype=BF16, tm=512):
    m, d = x.shape
    tm = _tile(m, tm)
    return pl.pallas_call(
        _rmsnorm_kernel,
        grid=(m // tm,),
        in_specs=[pl.BlockSpec((tm, d), lambda i: (i, 0)), pl.BlockSpec((1, d), lambda i: (0, 0))],
        out_specs=pl.BlockSpec((tm, d), lambda i: (i, 0)),
        out_shape=jax.ShapeDtypeStruct((m, d), out_dtype),
        compiler_params=_cparams("parallel"),
        name="rmsnorm",
    )(x, g.reshape(1, d))


def _linear_kernel(a_ref, w_ref, *o_refs, head_major):
    acc = _dot(a_ref[...].astype(BF16), w_ref[...])
    for o_ref in o_refs:
        if head_major and len(o_ref.shape) == 3:
            for hh in range(o_ref.shape[0]):
                o_ref[hh] = acc[:, hh * LANES:(hh + 1) * LANES].astype(o_ref.dtype)
        else:
            o_ref[...] = acc.astype(o_ref.dtype)


def _linear(a, w, col_off, n_out, out_dtypes, head_major=(), tm=1024, tn=512, name="linear"):
    m, k = a.shape
    tm, tn = _tile(m, tm), _tile(n_out, tn)
    assert col_off % tn == 0 and tn % LANES == 0
    joff = col_off // tn
    out_specs, out_shape = [], []
    for dt, hm in zip(out_dtypes, head_major or (False,) * len(out_dtypes)):
        if hm:
            out_specs.append(pl.BlockSpec((tn // LANES, tm, LANES), lambda i, j: (j, i, 0)))
            out_shape.append(jax.ShapeDtypeStruct((n_out // LANES, m, LANES), dt))
        else:
            out_specs.append(pl.BlockSpec((tm, tn), lambda i, j: (i, j)))
            out_shape.append(jax.ShapeDtypeStruct((m, n_out), dt))
    return pl.pallas_call(
        functools.partial(_linear_kernel, head_major=bool(head_major)),
        grid=(m // tm, n_out // tn),
        in_specs=[pl.BlockSpec((tm, k), lambda i, j: (i, 0)),
                  pl.BlockSpec((k, tn), lambda i, j: (0, joff + j))],
        out_specs=out_specs,
        out_shape=out_shape,
        compiler_params=_cparams("parallel", "arbitrary"),
        name=name,
    )(a, w)


def _linear_rope_kernel(a_ref, w_ref, cos_ref, sin_ref, o_ref, *, dk, n_q_tiles, k_scale):
    acc = _dot(a_ref[...], w_ref[...])
    scale = jnp.where(pl.program_id(1) >= n_q_tiles, k_scale, 1.0).astype(F32)
    cos, sin = cos_ref[...] * scale, sin_ref[...] * scale
    half = dk // 2
    for hh in range(acc.shape[1] // dk):
        x1 = acc[:, hh * dk:hh * dk + half]
        x2 = acc[:, hh * dk + half:(hh + 1) * dk]
        o_ref[:, hh * dk:hh * dk + half] = x1 * cos - x2 * sin
        o_ref[:, hh * dk + half:(hh + 1) * dk] = x1 * sin + x2 * cos


def _linear_rope(a, w, n_out, cos, sin, dk, n_q, k_scale, tm=1024, tn=512):
    m, k = a.shape
    ltab = cos.shape[0]
    tm = _tile(math.gcd(m, ltab), tm)
    tn = _tile(n_out, tn)
    ntab = ltab // tm
    return pl.pallas_call(
        functools.partial(_linear_rope_kernel, dk=dk, n_q_tiles=n_q // tn, k_scale=k_scale),
        grid=(m // tm, n_out // tn),
        in_specs=[pl.BlockSpec((tm, k), lambda i, j: (i, 0)),
                  pl.BlockSpec((k, tn), lambda i, j: (0, j)),
                  pl.BlockSpec((tm, dk // 2), lambda i, j: (i % ntab, 0)),
                  pl.BlockSpec((tm, dk // 2), lambda i, j: (i % ntab, 0))],
        out_specs=pl.BlockSpec((tm, tn), lambda i, j: (i, j)),
        out_shape=jax.ShapeDtypeStruct((m, n_out), F32),
        compiler_params=_cparams("parallel", "arbitrary"),
        name="ret_qk_rope",
    )(a, w, cos, sin)


def _outproj_kernel(a_ref, w_ref, x_ref, gp_ref, gn_ref, xo_ref, *rest, emit_h):
    acc_ref = rest[-1]
    kk = pl.program_id(1)

    @pl.when(kk == 0)
    def _():
        acc_ref[...] = jnp.zeros_like(acc_ref)

    acc_ref[...] += _dot(a_ref[...].astype(BF16), w_ref[...])

    @pl.when(kk == pl.num_programs(1) - 1)
    def _():
        xn = x_ref[...] + _rms(acc_ref[...], gp_ref[...])
        xo_ref[...] = xn
        if emit_h:
            rest[0][...] = _rms(xn, gn_ref[...]).astype(BF16)


def _outproj(a, w, x, g_post, g_next, emit_h=True, tm=512, tk=512, name="outproj"):
    m, k = a.shape
    d = w.shape[1]
    tm, tk = _tile(m, tm), _tile(k, tk)
    out_specs = [pl.BlockSpec((tm, d), lambda i, kk: (i, 0))]
    out_shape = [jax.ShapeDtypeStruct((m, d), F32)]
    if emit_h:
        out_specs.append(pl.BlockSpec((tm, d), lambda i, kk: (i, 0)))
        out_shape.append(jax.ShapeDtypeStruct((m, d), BF16))
    res = pl.pallas_call(
        functools.partial(_outproj_kernel, emit_h=emit_h),
        grid=(m // tm, k // tk),
        in_specs=[pl.BlockSpec((tm, tk), lambda i, kk: (i, kk)),
                  pl.BlockSpec((tk, d), lambda i, kk: (kk, 0)),
                  pl.BlockSpec((tm, d), lambda i, kk: (i, 0)),
                  pl.BlockSpec((1, d), lambda i, kk: (0, 0)),
                  pl.BlockSpec((1, d), lambda i, kk: (0, 0))],
        out_specs=out_specs,
        out_shape=out_shape,
        scratch_shapes=[pltpu.VMEM((tm, d), F32)],
        compiler_params=_cparams("parallel", "arbitrary"),
        name=name,
    )(a, w, x, g_post.reshape(1, d), g_next.reshape(1, d))
    return (res[0], res[1]) if emit_h else (res[0], None)


def _glu_outproj_kernel(a_ref, wv_ref, wg_ref, bv_ref, bg_ref, x_ref, gp_ref, gn_ref, xo_ref, ho_ref,
                        accv_ref, accg_ref):
    kk = pl.program_id(1)

    @pl.when(kk == 0)
    def _():
        accv_ref[...] = jnp.zeros_like(accv_ref)
        accg_ref[...] = jnp.zeros_like(accg_ref)

    a = a_ref[...]
    accv_ref[...] += _dot(a, wv_ref[...])
    accg_ref[...] += _dot(a, wg_ref[...])

    @pl.when(kk == pl.num_programs(1) - 1)
    def _():
        mix = (accv_ref[...] + bv_ref[...]) * jax.nn.sigmoid(accg_ref[...] + bg_ref[...])
        xn = x_ref[...] + _rms(mix, gp_ref[...])
        xo_ref[...] = xn
        ho_ref[...] = _rms(xn, gn_ref[...]).astype(BF16)


def _glu_outproj(a, w, b, x, g_post, g_next, tm=512, tk=512):
    m, k = a.shape
    d = w.shape[1] // 2
    tm, tk = _tile(m, tm), _tile(k, tk)
    row = lambda i, kk: (i, 0)
    const = lambda i, kk: (0, 0)
    return pl.pallas_call(
        _glu_outproj_kernel,
        grid=(m // tm, k // tk),
        in_specs=[pl.BlockSpec((tm, tk), lambda i, kk: (i, kk)),
                  pl.BlockSpec((tk, d), lambda i, kk: (kk, 0)),
                  pl.BlockSpec((tk, d), lambda i, kk: (kk, 1)),
                  pl.BlockSpec((1, d), lambda i, kk: (0, 0)),
                  pl.BlockSpec((1, d), lambda i, kk: (0, 1)),
                  pl.BlockSpec((tm, d), row), pl.BlockSpec((1, d), const), pl.BlockSpec((1, d), const)],
        out_specs=[pl.BlockSpec((tm, d), row), pl.BlockSpec((tm, d), row)],
        out_shape=[jax.ShapeDtypeStruct((m, d), F32), jax.ShapeDtypeStruct((m, d), BF16)],
        scratch_shapes=[pltpu.VMEM((tm, d), F32), pltpu.VMEM((tm, d), F32)],
        compiler_params=_cparams("parallel", "arbitrary"),
        name="s5_glu_outproj",
    )(a, w, w, b.reshape(1, 2 * d), b.reshape(1, 2 * d), x, g_post.reshape(1, d), g_next.reshape(1, d))


def _ffn_kernel(*refs, n_seq, has_state, emit_h, tiles_per_seq, nf):
    (h_ref, x_ref, wv_ref, wg_ref, cwv_ref, cwg_ref, cbv_ref, cbg_ref, wd_ref, gp_ref, gn_ref), refs = refs[:11], refs[11:]
    if has_state:
        (stv_ref, stg_ref), refs = refs[:2], refs[2:]
    xo_ref, refs = refs[0], refs[1:]
    if emit_h:
        ho_ref, refs = refs[0], refs[1:]
    cv_ref, cg_ref, acc_ref, act_even_ref, act_odd_ref = refs[:5]
    carry_refs = refs[5:]
    i, j = pl.program_id(0), pl.program_id(1)
    tm, tf = h_ref.shape[0], wv_ref.shape[1]
    seq = tm // n_seq

    if not has_state:
        @pl.when((i == 0) & (j == 0))
        def _():
            for carry_ref in carry_refs[:2]:
                carry_ref[...] = jnp.zeros_like(carry_ref)

    def conv_with_state(w_ref, cw_ref, cb_ref, st_ref, c_out_ref):
        up = _dot(h_ref[...], w_ref[...]).reshape(n_seq, seq, tf)
        prev1, prev2 = st_ref[:, 1:2, :], st_ref[:, 0:1, :]
        pos = lax.broadcasted_iota(jnp.int32, up.shape, 1)
        p1 = jnp.where(pos == 0, prev1, pltpu.roll(up, 1, axis=1))
        p2 = jnp.where(pos == 0, prev2, jnp.where(pos == 1, prev1, pltpu.roll(up, 2, axis=1)))
        c_out_ref[...] = up[:, seq - 2:, :]
        cw = cw_ref[...]
        c = cb_ref[...][None] + cw[0:1][None] * p2 + cw[1:2][None] * p1 + cw[2:3][None] * up
        return c.reshape(tm, tf)

    def conv_with_carry(w_ref, cw_ref, cb_ref, carry_ref, ext_ref, c_out_ref):
        up = _dot(h_ref[...], w_ref[...])
        ext_ref[0:8, :] = jnp.where(i % tiles_per_seq != 0, carry_ref[j], 0.0)
        ext_ref[8:, :] = up
        carry_ref[j] = up[tm - 8:, :]
        c_out_ref[0] = up[tm - 2:, :]
        cw = cw_ref[...]
        return cb_ref[...] + cw[0:1] * ext_ref[6:6 + tm, :] + cw[1:2] * ext_ref[7:7 + tm, :] + cw[2:3] * up

    def activation():
        if has_state:
            val = conv_with_state(wv_ref, cwv_ref, cbv_ref, stv_ref, cv_ref)
            gate = conv_with_state(wg_ref, cwg_ref, cbg_ref, stg_ref, cg_ref)
        else:
            val = conv_with_carry(wv_ref, cwv_ref, cbv_ref, carry_refs[0], carry_refs[2], cv_ref)
            gate = conv_with_carry(wg_ref, cwg_ref, cbg_ref, carry_refs[1], carry_refs[3], cg_ref)
        return (_gelu_tanh(gate) * val).astype(BF16)

    @pl.when(j == 0)
    def _():
        acc_ref[...] = jnp.zeros_like(acc_ref)
        act_odd_ref[...] = jnp.zeros_like(act_odd_ref)

    def step(act_new_ref, act_old_ref):
        act_new_ref[...] = activation()
        acc_ref[...] += _dot(act_old_ref[...], wd_ref[...])

    @pl.when((j < nf) & (j % 2 == 0))
    def _():
        step(act_even_ref, act_odd_ref)

    @pl.when((j < nf) & (j % 2 == 1))
    def _():
        step(act_odd_ref, act_even_ref)

    @pl.when(j == nf)
    def _():
        last = act_even_ref if (nf - 1) % 2 == 0 else act_odd_ref
        acc_ref[...] += _dot(last[...], wd_ref[...])
        xn = x_ref[...] + _rms(acc_ref[...], gp_ref[...])
        xo_ref[...] = xn
        if emit_h:
            ho_ref[...] = _rms(xn, gn_ref[...]).astype(BF16)


def _conv_ffn(h, x, w_up, conv_w, conv_b, w_down, g_post, g_next, seq_len, state=None, emit_h=True, tm=512):
    m, d = h.shape
    d_ff = w_down.shape[0]
    tf = _tile(d_ff, FFN_COLS)
    nf = d_ff // tf
    has_state = state is not None
    if has_state:
        tm = _tile(m // seq_len, max(tm // seq_len, 1)) * seq_len
        n_seq, tiles_per_seq = tm // seq_len, 1
    else:
        tm = _tile(seq_len, tm)
        n_seq, tiles_per_seq = 1, seq_len // tm
    n_seqs = m // seq_len
    row = lambda i, j: (i, 0)
    const = lambda i, j: (0, 0)
    up = lambda j: jnp.minimum(j, nf - 1)
    val_cols = lambda i, j: (0, up(j))
    gate_cols = lambda i, j: (0, nf + up(j))
    in_specs = [pl.BlockSpec((tm, d), row), pl.BlockSpec((tm, d), row),
                pl.BlockSpec((d, tf), val_cols), pl.BlockSpec((d, tf), gate_cols),
                pl.BlockSpec((CONV_W, tf), val_cols), pl.BlockSpec((CONV_W, tf), gate_cols),
                pl.BlockSpec((1, tf), val_cols), pl.BlockSpec((1, tf), gate_cols),
                pl.BlockSpec((tf, d), lambda i, j: (jnp.maximum(j - 1, 0), 0)),
                pl.BlockSpec((1, d), const), pl.BlockSpec((1, d), const)]
    args = [h, x, w_up, w_up, conv_w, conv_w, conv_b.reshape(1, -1), conv_b.reshape(1, -1), w_down,
            g_post.reshape(1, d), g_next.reshape(1, d)]
    if has_state:
        in_specs += [pl.BlockSpec((n_seq, CONV_W - 1, tf), lambda i, j: (i, 0, up(j))),
                     pl.BlockSpec((n_seq, CONV_W - 1, tf), lambda i, j: (i, 0, nf + up(j)))]
        args += [state, state]
        scratch = []
    else:
        scratch = [pltpu.VMEM((nf, 8, tf), F32), pltpu.VMEM((nf, 8, tf), F32),
                   pltpu.VMEM((tm + 8, tf), F32), pltpu.VMEM((tm + 8, tf), F32)]
    conv_map = lambda i, j: (i, 0, up(j))
    out_specs = [pl.BlockSpec((tm, d), row)]
    out_shape = [jax.ShapeDtypeStruct((m, d), F32)]
    if emit_h:
        out_specs.append(pl.BlockSpec((tm, d), row))
        out_shape.append(jax.ShapeDtypeStruct((m, d), BF16))
    out_specs += [pl.BlockSpec((n_seq, CONV_W - 1, tf), conv_map)] * 2
    out_shape += [jax.ShapeDtypeStruct((n_seqs * tiles_per_seq, CONV_W - 1, d_ff), F32)] * 2
    res = pl.pallas_call(
        functools.partial(_ffn_kernel, n_seq=n_seq, has_state=has_state, emit_h=emit_h, tiles_per_seq=tiles_per_seq,
                          nf=nf),
        grid=(m // tm, nf + 1),
        in_specs=in_specs,
        out_specs=out_specs,
        out_shape=out_shape,
        scratch_shapes=[pltpu.VMEM((tm, d), F32), pltpu.VMEM((tm, tf), BF16), pltpu.VMEM((tm, tf), BF16)] + scratch,
        compiler_params=_cparams("arbitrary", "arbitrary"),
        name="conv_ffn_state" if has_state else "conv_ffn",
    )(*args)
    x_new = res[0]
    h_new = res[1] if emit_h else None
    conv_state = jnp.concatenate(res[-2:], axis=-1)[tiles_per_seq - 1::tiles_per_seq]
    return x_new, h_new, conv_state


def _ret_kernel(*refs, has_s0):
    if has_s0:
        lg_ref, q_ref, k_ref, v_ref, g_ref, s0_ref, y_ref, so_ref, s_ref = refs
    else:
        lg_ref, q_ref, k_ref, v_ref, g_ref, y_ref, so_ref, s_ref = refs
    c = pl.program_id(2)
    cl = q_ref.shape[0]

    @pl.when(c == 0)
    def _():
        s_ref[...] = s0_ref[0, 0] if has_s0 else jnp.zeros_like(s_ref)

    lg = lg_ref[0, 0:1, 0:1]
    q, k, v = q_ref[...], k_ref[...], v_ref[...]
    row = lax.broadcasted_iota(jnp.int32, (cl, cl), 0)
    col = lax.broadcasted_iota(jnp.int32, (cl, cl), 1)
    diff = (row - col).astype(F32)
    decay = jnp.where(diff >= 0, jnp.exp(jnp.maximum(diff, 0.0) * lg), 0.0)
    scores = _dot_nt(q.astype(BF16), k.astype(BF16)) * decay
    o = _dot(scores.astype(BF16), v)
    idx = lax.broadcasted_iota(jnp.int32, (cl, 1), 0).astype(F32)
    s_prev = s_ref[...]
    o = o + _dot((q * jnp.exp((idx + 1.0) * lg)).astype(BF16), s_prev.astype(BF16))
    k_dec = k * jnp.exp((cl - 1.0 - idx) * lg)
    s_new = jnp.exp(cl * lg) * s_prev + _dot(k_dec.T.astype(BF16), v)
    s_ref[...] = s_new
    mu = jnp.mean(o, axis=-1, keepdims=True)
    var = jnp.mean(jnp.square(o - mu), axis=-1, keepdims=True)
    of = (o - mu) * lax.rsqrt(var + GN_EPS)
    g = g_ref[...]
    y_ref[...] = (g * jax.nn.sigmoid(g) * of).astype(y_ref.dtype)

    @pl.when(c == pl.num_programs(2) - 1)
    def _():
        so_ref[0, 0] = s_new


def _retention(qk, v, g, batch, heads, s0=None, chunk=256):
    m = qk.shape[0]
    dk, dv = qk.shape[1] // (2 * heads), v.shape[1] // heads
    seq = m // batch
    cl = _tile(seq, chunk)
    nc = seq // cl
    lg = jnp.log1p(-jnp.exp2(-5.0 - jnp.arange(heads, dtype=F32)))
    lg_tab = jnp.broadcast_to(lg[:, None, None], (heads, 8, LANES))
    tok = lambda b, h, c: (b * nc + c, h)
    in_specs = [pl.BlockSpec((1, 8, LANES), lambda b, h, c: (h, 0, 0)),
                pl.BlockSpec((cl, dk), tok), pl.BlockSpec((cl, dk), lambda b, h, c: (b * nc + c, heads + h)),
                pl.BlockSpec((cl, dv), tok), pl.BlockSpec((cl, dv), tok)]
    args = [lg_tab, qk, qk, v, g]
    if s0 is not None:
        in_specs.append(pl.BlockSpec((1, 1, dk, dv), lambda b, h, c: (b, h, 0, 0)))
        args.append(s0)
    return pl.pallas_call(
        functools.partial(_ret_kernel, has_s0=s0 is not None),
        grid=(batch, heads, nc),
        in_specs=in_specs,
        out_specs=[pl.BlockSpec((cl, dv), tok), pl.BlockSpec((1, 1, dk, dv), lambda b, h, c: (b, h, 0, 0))],
        out_shape=[jax.ShapeDtypeStruct((m, heads * dv), BF16), jax.ShapeDtypeStruct((batch, heads, dk, dv), F32)],
        scratch_shapes=[pltpu.VMEM((dk, dv), F32)],
        compiler_params=_cparams("parallel", "parallel", "arbitrary"),
        name="retention",
    )(*args)


def _mla_down_kernel(h_ref, wq_ref, wkv_ref, wr_ref, wrr_ref, gq_ref, gkv_ref, cos_ref, sin_ref,
                     cq_ref, ckv_ref, kr_ref):
    h = h_ref[...]
    cq_ref[...] = _rms(_dot(h, wq_ref[...]), gq_ref[...]).astype(cq_ref.dtype)
    ckv_ref[...] = _rms(_dot(h, wkv_ref[...]), gkv_ref[...])
    kr_ref[...] = _dot(h, wr_ref[...]) * cos_ref[...] + _dot(h, wrr_ref[...]) * sin_ref[...]


def _mla_down(h, wq, wkv, wr, wrr, gq, gkv, cos, sin, tm=512):
    m, d = h.shape
    ql, kvl, rd = wq.shape[1], wkv.shape[1], wr.shape[1]
    ltab = cos.shape[0]
    tm = _tile(math.gcd(m, ltab), tm)
    ntab = ltab // tm
    const = lambda i: (0, 0)
    row = lambda i: (i, 0)
    tab = lambda i: (i % ntab, 0)
    return pl.pallas_call(
        _mla_down_kernel,
        grid=(m // tm,),
        in_specs=[pl.BlockSpec((tm, d), row), pl.BlockSpec((d, ql), const), pl.BlockSpec((d, kvl), const),
                  pl.BlockSpec((d, rd), const), pl.BlockSpec((d, rd), const),
                  pl.BlockSpec((1, ql), const), pl.BlockSpec((1, kvl), const),
                  pl.BlockSpec((tm, rd), tab), pl.BlockSpec((tm, rd), tab)],
        out_specs=[pl.BlockSpec((tm, ql), row), pl.BlockSpec((tm, kvl), row), pl.BlockSpec((tm, rd), row)],
        out_shape=[jax.ShapeDtypeStruct((m, ql), BF16), jax.ShapeDtypeStruct((m, kvl), F32),
                   jax.ShapeDtypeStruct((m, rd), F32)],
        compiler_params=_cparams("parallel"),
        name="mla_down",
    )(h, wq, wkv, wr, wrr, gq.reshape(1, ql), gkv.reshape(1, kvl), cos, sin)


def _mla_q_kernel(cq_ref, wn_ref, wr_ref, wrr_ref, cos_ref, sin_ref, q_ref, *, scale):
    cq = cq_ref[...]
    qn = _dot(cq, wn_ref[...]) * scale
    nope = LANES
    for hh in range(q_ref.shape[0]):
        q_ref[hh, :, :nope] = qn[:, hh * LANES:(hh + 1) * LANES].astype(q_ref.dtype)
    a, b = _dot(cq, wr_ref[...]), _dot(cq, wrr_ref[...])
    cos, sin = cos_ref[...] * scale, sin_ref[...] * scale
    rd = q_ref.shape[2] - nope
    per = LANES // rd
    for p in range(a.shape[1] // LANES):
        r = a[:, p * LANES:(p + 1) * LANES] * cos + b[:, p * LANES:(p + 1) * LANES] * sin
        for s in range(per):
            q_ref[p * per + s, :, nope:] = r[:, s * rd:(s + 1) * rd].astype(q_ref.dtype)


def _mla_q(cq, wn, wr, wrr, cos, sin, heads, scale, tm=512):
    m, ql = cq.shape
    nope, rd = wn.shape[1] // heads, wr.shape[1] // heads
    assert nope == LANES and LANES % rd == 0
    ltab = cos.shape[0]
    tm = _tile(math.gcd(m, ltab), tm)
    ntab = ltab // tm
    const = lambda i: (0, 0)
    tab = lambda i: (i % ntab, 0)
    return pl.pallas_call(
        functools.partial(_mla_q_kernel, scale=scale),
        grid=(m // tm,),
        in_specs=[pl.BlockSpec((tm, ql), lambda i: (i, 0)), pl.BlockSpec(wn.shape, const),
                  pl.BlockSpec(wr.shape, const), pl.BlockSpec(wrr.shape, const),
                  pl.BlockSpec((tm, LANES), tab), pl.BlockSpec((tm, LANES), tab)],
        out_specs=pl.BlockSpec((heads, tm, nope + rd), lambda i: (0, i, 0)),
        out_shape=jax.ShapeDtypeStruct((heads, m, nope + rd), BF16),
        compiler_params=_cparams("parallel"),
        name="mla_q",
    )(cq, wn, wr, wrr, cos, sin)


def _mla_kv_kernel(ckv_ref, kr_ref, wk_ref, wv_ref, k_ref, v_ref):
    ckv = ckv_ref[...].astype(BF16)
    kn, v = _dot(ckv, wk_ref[...]), _dot(ckv, wv_ref[...])
    kr = kr_ref[...].astype(k_ref.dtype)
    nope = k_ref.shape[2] - kr.shape[1]
    vd = v_ref.shape[2]
    for hh in range(k_ref.shape[0]):
        k_ref[hh, :, :nope] = kn[:, hh * nope:(hh + 1) * nope].astype(k_ref.dtype)
        k_ref[hh, :, nope:] = kr
        v_ref[hh] = v[:, hh * vd:(hh + 1) * vd].astype(v_ref.dtype)


def _mla_kv(ckv, kr, wk, wv, heads, tm=512):
    m, kvl = ckv.shape
    rd = kr.shape[1]
    nope, vd = wk.shape[1] // heads, wv.shape[1] // heads
    tm = _tile(m, tm)
    const = lambda i: (0, 0)
    return pl.pallas_call(
        _mla_kv_kernel,
        grid=(m // tm,),
        in_specs=[pl.BlockSpec((tm, kvl), lambda i: (i, 0)), pl.BlockSpec((tm, rd), lambda i: (i, 0)),
                  pl.BlockSpec(wk.shape, const), pl.BlockSpec(wv.shape, const)],
        out_specs=[pl.BlockSpec((heads, tm, nope + rd), lambda i: (0, i, 0)),
                   pl.BlockSpec((heads, tm, vd), lambda i: (0, i, 0))],
        out_shape=[jax.ShapeDtypeStruct((heads, m, nope + rd), BF16), jax.ShapeDtypeStruct((heads, m, vd), BF16)],
        compiler_params=_cparams("parallel"),
        name="mla_kv",
    )(ckv, kr, wk, wv)


def _mla_flash_kernel(q_ref, k_ref, v_ref, o_ref, m_ref, l_ref, acc_ref):
    qi, ki = pl.program_id(1), pl.program_id(2)
    nh, tq, vd = acc_ref.shape
    tk = k_ref.shape[1]

    @pl.when(ki == 0)
    def _():
        m_ref[...] = jnp.full_like(m_ref, -jnp.inf)
        l_ref[...] = jnp.zeros_like(l_ref)
        acc_ref[...] = jnp.zeros_like(acc_ref)

    def sweep(diagonal):
        if diagonal:
            row_chunk = lax.broadcasted_iota(jnp.int32, (tq, tk), 0) // CHUNK
            col_chunk = lax.broadcasted_iota(jnp.int32, (tq, tk), 1) // CHUNK
            mask = col_chunk <= row_chunk

        def head(hh, carry):
            s = _dot_nt(q_ref[hh], k_ref[hh])
            if diagonal:
                s = jnp.where(mask, s, -jnp.inf)
            m_prev = m_ref[hh]
            m_new = jnp.maximum(m_prev, jnp.max(s, axis=-1, keepdims=True))
            alpha = jnp.exp(m_prev - m_new)
            p = jnp.concatenate([jnp.exp(s[:, c * LANES:(c + 1) * LANES] - m_new) for c in range(tk // LANES)],
                                axis=-1)
            l_ref[hh] = alpha * l_ref[hh] + jnp.sum(p, axis=-1, keepdims=True)
            acc_ref[hh] = alpha * acc_ref[hh] + _dot(p.astype(BF16), v_ref[hh])
            m_ref[hh] = m_new
            return carry

        lax.fori_loop(0, nh, head, 0, unroll=8)

    @pl.when(ki < qi)
    def _():
        sweep(False)

    @pl.when(ki == qi)
    def _():
        sweep(True)
        for hh in range(nh):
            o_ref[:, hh * vd:(hh + 1) * vd] = (acc_ref[hh] / l_ref[hh]).astype(o_ref.dtype)


def _mla_flash(q, k, v, batch, t=512):
    heads, m, qd = q.shape
    vd = v.shape[2]
    seq = m // batch
    t = _tile(seq, t)
    assert t % CHUNK == 0 and vd == LANES
    nq = seq // t
    qmap = lambda b, qi, ki: (0, b * nq + qi, 0)
    kmap = lambda b, qi, ki: (0, b * nq + jnp.minimum(ki, qi), 0)
    return pl.pallas_call(
        _mla_flash_kernel,
        grid=(batch, nq, nq),
        in_specs=[pl.BlockSpec((heads, t, qd), qmap), pl.BlockSpec((heads, t, qd), kmap),
                  pl.BlockSpec((heads, t, vd), kmap)],
        out_specs=pl.BlockSpec((t, heads * vd), lambda b, qi, ki: (b * nq + qi, 0)),
        out_shape=jax.ShapeDtypeStruct((m, heads * vd), BF16),
        scratch_shapes=[pltpu.VMEM((heads, t, LANES), F32), pltpu.VMEM((heads, t, LANES), F32),
                        pltpu.VMEM((heads, t, vd), F32)],
        compiler_params=_cparams("parallel", "parallel", "arbitrary"),
        name="mla_flash",
    )(q, k, v)


def _mla_sample_kernel(q_ref, ckv_ref, kr_ref, cckv_ref, ckr_ref, wuk_ref, wuv_ref, o_ref, ql_ref):
    nh, t, qd = q_ref.shape
    kvl = ckv_ref.shape[1]
    nope, vd = wuk_ref.shape[2], wuv_ref.shape[2]
    for hh in range(nh):
        ql_ref[hh] = _dot_nt(q_ref[hh, :, :nope], wuk_ref[hh]).astype(BF16)
    q_lat = ql_ref[...].reshape(nh * t, kvl)
    q_rope = q_ref[:, :, nope:].reshape(nh * t, qd - nope)
    c_old, r_old = cckv_ref[0].astype(BF16), ckr_ref[0].astype(BF16)
    c_new, r_new = ckv_ref[...].astype(BF16), kr_ref[...].astype(BF16)
    s_old = _dot_nt(q_lat, c_old) + _dot_nt(q_rope, r_old)
    s_new = _dot_nt(q_lat, c_new) + _dot_nt(q_rope, r_new)
    mx = jnp.maximum(jnp.max(s_old, axis=-1, keepdims=True), jnp.max(s_new, axis=-1, keepdims=True))
    p_old, p_new = jnp.exp(s_old - mx), jnp.exp(s_new - mx)
    denom = jnp.sum(p_old, axis=-1, keepdims=True) + jnp.sum(p_new, axis=-1, keepdims=True)
    o_lat = (_dot(p_old.astype(BF16), c_old) + _dot(p_new.astype(BF16), c_new)) / denom
    o_lat = o_lat.astype(BF16)
    for hh in range(nh):
        o_ref[:, hh * vd:(hh + 1) * vd] = _dot(o_lat[hh * t:(hh + 1) * t], wuv_ref[hh]).astype(o_ref.dtype)


def _mla_sample(q, ckv, kr, cache_ckv, cache_kr, wuk_t, wuv_t):
    heads, m, qd = q.shape
    batch, past, kvl = cache_ckv.shape
    t = m // batch
    rd = kr.shape[1]
    vd = wuv_t.shape[2]
    const3 = lambda b: (0, 0, 0)
    return pl.pallas_call(
        _mla_sample_kernel,
        grid=(batch,),
        in_specs=[pl.BlockSpec((heads, t, qd), lambda b: (0, b, 0)),
                  pl.BlockSpec((t, kvl), lambda b: (b, 0)), pl.BlockSpec((t, rd), lambda b: (b, 0)),
                  pl.BlockSpec((1, past, kvl), lambda b: (b, 0, 0)), pl.BlockSpec((1, past, rd), lambda b: (b, 0, 0)),
                  pl.BlockSpec(wuk_t.shape, const3), pl.BlockSpec(wuv_t.shape, const3)],
        out_specs=pl.BlockSpec((t, heads * vd), lambda b: (b, 0)),
        out_shape=jax.ShapeDtypeStruct((m, heads * vd), BF16),
        scratch_shapes=[pltpu.VMEM((heads, t, kvl), BF16)],
        compiler_params=_cparams("parallel"),
        name="mla_sample",
    )(q, ckv, kr, cache_ckv, cache_kr, wuk_t, wuv_t)


def _s5_kernel(*refs, has_h0, col_block):
    (x_ref, g_ref, wbr_ref, wbi_ref, ar_ref, ai_ref, cr_ref, ci_ref, d_ref), refs = refs[:9], refs[9:]
    if has_h0:
        (h0r_ref, h0i_ref), refs = refs[:2], refs[2:]
    y_ref, sr_ref, si_ref, br_ref, bi_ref, hr_ref, hi_ref = refs
    tt = pl.program_id(1)
    t, d = x_ref.shape
    n_state = br_ref.shape[1]
    n_gb, gin, gst = wbr_ref.shape

    @pl.when(tt == 0)
    def _():
        if has_h0:
            hr_ref[...] = h0r_ref[0]
            hi_ref[...] = h0i_ref[0]
        else:
            hr_ref[...] = jnp.zeros_like(hr_ref)
            hi_ref[...] = jnp.zeros_like(hi_ref)

    h = _rms(x_ref[...], g_ref[...])
    ub = h.astype(BF16)
    for gb in range(n_gb):
        u_blk = ub[:, gb * gin:(gb + 1) * gin]
        br_ref[:, gb * gst:(gb + 1) * gst] = _dot(u_blk, wbr_ref[gb])
        bi_ref[:, gb * gst:(gb + 1) * gst] = _dot(u_blk, wbi_ref[gb])

    for cb in range(n_state // col_block):
        sl = slice(cb * col_block, (cb + 1) * col_block)
        a_re, a_im = ar_ref[:, sl], ai_ref[:, sl]

        def step(s, carry):
            h_re, h_im = carry
            n_re = a_re * h_re - a_im * h_im + br_ref[pl.ds(s, 1), sl]
            n_im = a_re * h_im + a_im * h_re + bi_ref[pl.ds(s, 1), sl]
            br_ref[pl.ds(s, 1), sl] = n_re
            bi_ref[pl.ds(s, 1), sl] = n_im
            return n_re, n_im

        h_re, h_im = lax.fori_loop(0, t, step, (hr_ref[:, sl], hi_ref[:, sl]))
        hr_ref[:, sl] = h_re
        hi_ref[:, sl] = h_im

    for gb in range(n_gb):
        hs_re = br_ref[:, gb * gst:(gb + 1) * gst].astype(BF16)
        hs_im = bi_ref[:, gb * gst:(gb + 1) * gst].astype(BF16)
        cs = slice(gb * gin, (gb + 1) * gin)
        y = _dot(hs_re, cr_ref[gb]) - _dot(hs_im, ci_ref[gb]) + d_ref[:, cs] * h[:, cs]
        y_ref[:, cs] = _gelu_tanh(y).astype(y_ref.dtype)

    @pl.when(tt == pl.num_programs(1) - 1)
    def _():
        sr_ref[0] = hr_ref[...]
        si_ref[0] = hi_ref[...]


def _s5(x, g, wb_re, wb_im, a_re, a_im, wc_re, wc_im, d_skip, batch, h0=None, t=256, col_block=4096):
    m, d = x.shape
    seq = m // batch
    t = _tile(seq, t)
    nt = seq // t
    n_state = a_re.shape[1]
    col_block = _tile(n_state, col_block)
    const2 = lambda b, tt: (0, 0)
    const3 = lambda b, tt: (0, 0, 0)
    in_specs = [pl.BlockSpec((t, d), lambda b, tt: (b * nt + tt, 0)), pl.BlockSpec((1, d), const2),
                pl.BlockSpec(wb_re.shape, const3), pl.BlockSpec(wb_im.shape, const3),
                pl.BlockSpec((1, n_state), const2), pl.BlockSpec((1, n_state), const2),
                pl.BlockSpec(wc_re.shape, const3), pl.BlockSpec(wc_im.shape, const3),
                pl.BlockSpec((1, d), const2)]
    args = [x, g.reshape(1, d), wb_re, wb_im, a_re, a_im, wc_re, wc_im, d_skip.reshape(1, d)]
    if h0 is not None:
        in_specs += [pl.BlockSpec((1, 1, n_state), lambda b, tt: (b, 0, 0))] * 2
        args += [h0[0].reshape(batch, 1, n_state), h0[1].reshape(batch, 1, n_state)]
    state_spec = pl.BlockSpec((1, 1, n_state), lambda b, tt: (b, 0, 0))
    return pl.pallas_call(
        functools.partial(_s5_kernel, has_h0=h0 is not None, col_block=col_block),
        grid=(batch, nt),
        in_specs=in_specs,
        out_specs=[pl.BlockSpec((t, d), lambda b, tt: (b * nt + tt, 0)), state_spec, state_spec],
        out_shape=[jax.ShapeDtypeStruct((m, d), BF16), jax.ShapeDtypeStruct((batch, 1, n_state), F32),
                   jax.ShapeDtypeStruct((batch, 1, n_state), F32)],
        scratch_shapes=[pltpu.VMEM((t, n_state), F32), pltpu.VMEM((t, n_state), F32),
                        pltpu.VMEM((1, n_state), F32), pltpu.VMEM((1, n_state), F32)],
        compiler_params=_cparams("parallel", "arbitrary"),
        name="s5_scan",
    )(*args)


def _s5_params(a_re, a_im, log_dt, b_re, b_im, c_re, c_im, groups_per_block=8):
    g, p, gc = b_re.shape
    step = jnp.exp(log_dt)[:, None]
    xr, xi = a_re * step, a_im * step
    ab_re, ab_im = jnp.exp(xr) * jnp.cos(xi), jnp.exp(xr) * jnp.sin(xi)
    m1_re = jnp.expm1(xr) * jnp.cos(xi) - 2.0 * jnp.square(jnp.sin(0.5 * xi))
    den = a_re * a_re + a_im * a_im
    f_re = (m1_re * a_re + ab_im * a_im) / den
    f_im = (ab_im * a_re - m1_re * a_im) / den
    bb_re = f_re[..., None] * b_re - f_im[..., None] * b_im
    bb_im = f_re[..., None] * b_im + f_im[..., None] * b_re
    gpb = groups_per_block
    eye = jnp.eye(gpb, dtype=F32)

    def blockdiag_b(w):
        w = w.reshape(g // gpb, gpb, p, gc)
        return jnp.einsum('bgpc,gh->bgchp', w, eye).reshape(g // gpb, gpb * gc, gpb * p).astype(BF16)

    def blockdiag_c(w):
        w = w.reshape(g // gpb, gpb, gc, p)
        return jnp.einsum('bgcp,gh->bgphc', w, eye).reshape(g // gpb, gpb * p, gpb * gc).astype(BF16)

    return (blockdiag_b(bb_re), blockdiag_b(bb_im), ab_re.reshape(1, g * p), ab_im.reshape(1, g * p),
            blockdiag_c(c_re), blockdiag_c(c_im))


def _sb_block(q, k, v, mask, tri, r_prev, scale):
    z = _dot_nt(q, k) * scale
    sp = _softplus(z)
    if mask is not None:
        sp = jnp.where(mask, sp, 0.0)
    hi = sp.astype(BF16)
    lo = (sp - hi.astype(F32)).astype(BF16)
    csum = _dot(hi, tri) + _dot(lo, tri)
    log_w = z - csum
    if mask is not None:
        log_w = jnp.where(mask, log_w, -jnp.inf)
    n_keys = z.shape[1]
    if n_keys % LANES:
        w = jnp.exp(log_w - r_prev[:, :n_keys])
    else:
        w = jnp.concatenate([jnp.exp(log_w[:, c * LANES:(c + 1) * LANES] - r_prev) for c in range(n_keys // LANES)],
                            axis=-1)
    return _dot(w.astype(BF16), v), r_prev + jnp.sum(sp, axis=-1, keepdims=True)


def _tri(n):
    row = lax.broadcasted_iota(jnp.int32, (n, n), 0)
    col = lax.broadcasted_iota(jnp.int32, (n, n), 1)
    return (row >= col).astype(BF16)


def _sb_flash_kernel(q_ref, k_hbm, v_hbm, o_ref, kbuf, vbuf, sem, r_ref, acc_ref, *, scale, seq):
    b, qi = pl.program_id(0), pl.program_id(1)
    nh, t, hd = acc_ref.shape
    row0 = b * seq + qi * t

    def copies(kj, slot):
        start = pl.multiple_of(row0 - kj * t, t)
        return (pltpu.make_async_copy(k_hbm.at[:, pl.ds(start, t), :], kbuf.at[slot], sem.at[0, slot]),
                pltpu.make_async_copy(v_hbm.at[:, pl.ds(start, t), :], vbuf.at[slot], sem.at[1, slot]))

    for cp in copies(0, 0):
        cp.start()
    r_ref[...] = jnp.zeros_like(r_ref)
    acc_ref[...] = jnp.zeros_like(acc_ref)
    row = lax.broadcasted_iota(jnp.int32, (t, t), 0)
    col = lax.broadcasted_iota(jnp.int32, (t, t), 1)
    tri = _tri(t)

    def sweep(carry):
        kj, _ = carry
        slot = kj % 2
        for cp in copies(kj, slot):
            cp.wait()

        @pl.when(kj < qi)
        def _():
            for cp in copies(kj + 1, 1 - slot):
                cp.start()

        mask = (col < row) | (kj > 0)

        def head(hh, r_min):
            o, r = _sb_block(q_ref[hh], kbuf[slot, hh], vbuf[slot, hh], mask, tri, r_ref[hh], scale)
            acc_ref[hh] += o
            r_ref[hh] = r
            return jnp.minimum(r_min, jnp.min(r))

        r_min = lax.fori_loop(0, nh, head, jnp.float32(jnp.inf), unroll=8)
        return kj + 1, (r_min > SB_EXIT_SUM).astype(jnp.int32)

    kj_end, _ = lax.while_loop(lambda c: (c[0] <= qi) & (c[1] == 0), sweep, (jnp.int32(0), jnp.int32(0)))

    @pl.when(kj_end <= qi)
    def _():
        for cp in copies(kj_end, kj_end % 2):
            cp.wait()

    for hh in range(nh):
        o_ref[:, hh * hd:(hh + 1) * hd] = acc_ref[hh].astype(o_ref.dtype)


def _sb_flash(q, k, v, batch, scale, t=256):
    heads, m, hd = q.shape
    seq = m // batch
    t = _tile(seq, t)
    nq = seq // t
    return pl.pallas_call(
        functools.partial(_sb_flash_kernel, scale=scale, seq=seq),
        grid=(batch, nq),
        in_specs=[pl.BlockSpec((heads, t, hd), lambda b, qi: (0, b * nq + qi, 0)),
                  pl.BlockSpec(memory_space=pl.ANY), pl.BlockSpec(memory_space=pl.ANY)],
        out_specs=pl.BlockSpec((t, heads * hd), lambda b, qi: (b * nq + qi, 0)),
        out_shape=jax.ShapeDtypeStruct((m, heads * hd), BF16),
        scratch_shapes=[pltpu.VMEM((2, heads, t, hd), BF16), pltpu.VMEM((2, heads, t, hd), BF16),
                        pltpu.SemaphoreType.DMA((2, 2)),
                        pltpu.VMEM((heads, t, LANES), F32), pltpu.VMEM((heads, t, hd), F32)],
        compiler_params=_cparams("parallel", "parallel"),
        name="sb_flash",
    )(q, k, v)


def _sb_sample_kernel(q_ref, kn_ref, vn_ref, kc_hbm, vc_hbm, o_ref, kbuf, vbuf, sem, r_ref, acc_ref, *,
                      scale, layer, blk):
    b = pl.program_id(0)
    nh, t, hd = acc_ref.shape
    nblk = kc_hbm.shape[2] // blk

    def copies(kb, slot):
        start = pl.multiple_of((nblk - 1 - kb) * blk, blk)
        cps = []
        for hh in range(nh):
            cps.append(pltpu.make_async_copy(kc_hbm.at[layer, b, pl.ds(start, blk), hh], kbuf.at[slot, hh],
                                             sem.at[0, slot]))
            cps.append(pltpu.make_async_copy(vc_hbm.at[layer, b, pl.ds(start, blk), hh], vbuf.at[slot, hh],
                                             sem.at[1, slot]))
        return cps

    for cp in copies(0, 0):
        cp.start()
    row = lax.broadcasted_iota(jnp.int32, (t, t), 0)
    col = lax.broadcasted_iota(jnp.int32, (t, t), 1)
    own_mask, own_tri = col < row, _tri(t)

    def own(hh, carry):
        o, r = _sb_block(q_ref[hh], kn_ref[hh], vn_ref[hh], own_mask, own_tri, jnp.zeros((t, LANES), F32), scale)
        acc_ref[hh] = o
        r_ref[hh] = r
        return carry

    lax.fori_loop(0, nh, own, 0)
    tri = _tri(blk)

    def sweep(carry):
        kb, _ = carry
        slot = kb % 2
        for cp in copies(kb, slot):
            cp.wait()

        @pl.when(kb + 1 < nblk)
        def _():
            for cp in copies(kb + 1, 1 - slot):
                cp.start()

        def head(hh, r_min):
            o, r = _sb_block(q_ref[hh], kbuf[slot, hh].astype(BF16), vbuf[slot, hh].astype(BF16), None, tri,
                             r_ref[hh], scale)
            acc_ref[hh] += o
            r_ref[hh] = r
            return jnp.minimum(r_min, jnp.min(r))

        r_min = lax.fori_loop(0, nh, head, jnp.float32(jnp.inf))
        return kb + 1, (r_min > SB_EXIT_SUM).astype(jnp.int32)

    kb_end, _ = lax.while_loop(lambda c: (c[0] < nblk) & (c[1] == 0), sweep, (jnp.int32(0), jnp.int32(0)))

    @pl.when(kb_end < nblk)
    def _():
        for cp in copies(kb_end, kb_end % 2):
            cp.wait()

    for hh in range(nh):
        o_ref[:, hh * hd:(hh + 1) * hd] = acc_ref[hh].astype(o_ref.dtype)


def _sb_sample(q, k_new, v_new, cache_k, cache_v, layer, scale, blk=256):
    heads, m, hd = q.shape
    batch, past = cache_k.shape[1], cache_k.shape[2]
    t = m // batch
    blk = _tile(past, blk)
    new = lambda b: (0, b, 0)
    return pl.pallas_call(
        functools.partial(_sb_sample_kernel, scale=scale, layer=layer, blk=blk),
        grid=(batch,),
        in_specs=[pl.BlockSpec((heads, t, hd), new), pl.BlockSpec((heads, t, hd), new),
                  pl.BlockSpec((heads, t, hd), new),
                  pl.BlockSpec(memory_space=pl.ANY), pl.BlockSpec(memory_space=pl.ANY)],
        out_specs=pl.BlockSpec((t, heads * hd), lambda b: (b, 0)),
        out_shape=jax.ShapeDtypeStruct((m, heads * hd), BF16),
        scratch_shapes=[pltpu.VMEM((2, heads, blk, hd), F32), pltpu.VMEM((2, heads, blk, hd), F32),
                        pltpu.SemaphoreType.DMA((2, 2)),
                        pltpu.VMEM((heads, t, LANES), F32), pltpu.VMEM((heads, t, hd), F32)],
        compiler_params=_cparams("parallel"),
        name="sb_sample",
    )(q, k_new, v_new, cache_k, cache_v)


def _rope_tables(pos, d, reps):
    half = d // 2
    inv = ROPE_BASE ** (-jnp.arange(half, dtype=F32) * 2.0 / d)
    ang = pos.astype(F32)[:, None] * inv[None, :]
    return jnp.tile(jnp.cos(ang), (1, reps)), jnp.tile(jnp.sin(ang), (1, reps))


def _rot_half_cols(w, d):
    k, n = w.shape
    w = w.reshape(k, n // d, d)
    half = d // 2
    return jnp.concatenate([-w[..., half:], w[..., :half]], axis=-1).reshape(k, n)


def kernel(x_prompt, x_sample, state_ret, cache_mla_ckv, cache_mla_krope, state_s5_re, state_s5_im, cache_sb_k, cache_sb_v, state_ffn_conv, norm_gains, ret_w_in, ret_w_out, mla_w_down, mla_g_q, mla_g_kv, mla_w_uq, mla_w_uk, mla_w_uv, mla_w_o, s5_a_re, s5_a_im, s5_log_dt, s5_b_re, s5_b_im, s5_c_re, s5_c_im, s5_d, s5_w_glu, s5_b_glu, sb_w_qkv, sb_w_o, ffn_w_up, ffn_conv_w, ffn_conv_b, ffn_w_down):
    n_p, seq_p, d = x_prompt.shape
    n_s, seq_s, _ = x_sample.shape
    depth = norm_gains.shape[0]
    n_mixers = 4
    streams = (dict(batch=n_p, seq=seq_p, pos=jnp.arange(seq_p)),
               dict(batch=n_s, seq=seq_s, pos=None))
    xs = [x_prompt.reshape(n_p * seq_p, d), x_sample.reshape(n_s * seq_s, d)]
    hs = [_rmsnorm(x, norm_gains[0, 0]) for x in xs]
    outs = {name: ([], []) for name in ('ret', 'ckv', 'kr', 's5re', 's5im', 'sbk', 'sbv', 'conv')}

    past_len = cache_mla_ckv.shape[2]

    def positions(si):
        if si == 0:
            return jnp.arange(seq_p)
        return jnp.tile(past_len + jnp.arange(seq_s), n_s)

    for i in range(depth):
        j, kind = i // n_mixers, i % n_mixers
        g = norm_gains[i]
        g_next = norm_gains[i + 1, 0] if i + 1 < depth else g[0]
        emit_next = i + 1 < depth and (i + 1) % n_mixers != 2
        ffn_up = ffn_w_up[i].astype(BF16)
        ffn_down = ffn_w_down[i].astype(BF16)
        for si, st in enumerate(streams):
            x, h, batch = xs[si], hs[si], st['batch']
            if kind == 0:
                heads, dk, dv = state_ret.shape[2], state_ret.shape[3], state_ret.shape[4]
                qk_w, v_w = heads * dk, heads * dv
                w_in = ret_w_in[j].astype(BF16)
                cos, sin = _rope_tables(positions(si), dk, 1)
                qk = _linear_rope(h, w_in, 2 * qk_w, cos, sin, dk, qk_w, dk ** -0.5)
                v, = _linear(h, w_in, 2 * qk_w, v_w, (BF16,), name="ret_v")
                gate, = _linear(h, w_in, 2 * qk_w + v_w, v_w, (F32,), name="ret_g")
                s0 = state_ret[j] if si == 1 else None
                y, s1 = _retention(qk, v, gate, batch, heads, s0)
                outs['ret'][si].append(s1)
                x, h = _outproj(y, ret_w_out[j].astype(BF16), x, g[1], g[2], name="ret_outproj")
            elif kind == 1:
                ql, kvl = mla_g_q.shape[1], mla_g_kv.shape[1]
                heads, nope = mla_w_uk.shape[2], mla_w_uk.shape[3]
                rd, vd = cache_mla_krope.shape[3], mla_w_uv.shape[3]
                past = cache_mla_ckv.shape[2]
                scale = (nope + rd) ** -0.5
                w_down = mla_w_down[j]
                w_kr = w_down[:, ql + kvl:]
                pos = positions(si)
                cos_k, sin_k = _rope_tables(pos, rd, 2)
                cos_q, sin_q = _rope_tables(pos, rd, 2 * LANES // rd)
                cq, ckv, kr = _mla_down(h, w_down[:, :ql].astype(BF16), w_down[:, ql:ql + kvl].astype(BF16),
                                        w_kr.astype(BF16), _rot_half_cols(w_kr, rd).astype(BF16),
                                        mla_g_q[j], mla_g_kv[j], cos_k, sin_k)
                w_uq = mla_w_uq[j].reshape(ql, heads, nope + rd)
                w_qn = w_uq[:, :, :nope].reshape(ql, heads * nope)
                w_qr = w_uq[:, :, nope:].reshape(ql, heads * rd)
                q = _mla_q(cq, w_qn.astype(BF16), w_qr.astype(BF16), _rot_half_cols(w_qr, rd).astype(BF16),
                           cos_q, sin_q, heads, scale)
                outs['ckv'][si].append(ckv.reshape(batch, st['seq'], kvl))
                outs['kr'][si].append(kr.reshape(batch, st['seq'], rd))
                if si == 0:
                    k, v = _mla_kv(ckv, kr, mla_w_uk[j].reshape(kvl, heads * nope).astype(BF16),
                                   mla_w_uv[j].reshape(kvl, heads * vd).astype(BF16), heads)
                    o = _mla_flash(q, k, v, batch)
                else:
                    o = _mla_sample(q, ckv, kr, cache_mla_ckv[j], cache_mla_krope[j],
                                    jnp.transpose(mla_w_uk[j], (1, 0, 2)).astype(BF16),
                                    jnp.transpose(mla_w_uv[j], (1, 0, 2)).astype(BF16))
                x, h = _outproj(o, mla_w_o[j].astype(BF16), x, g[1], g[2], name="mla_outproj")
            elif kind == 2:
                params = _s5_params(s5_a_re[j], s5_a_im[j], s5_log_dt[j], s5_b_re[j], s5_b_im[j],
                                    s5_c_re[j], s5_c_im[j])
                h0 = (state_s5_re[j], state_s5_im[j]) if si == 1 else None
                y, s_re, s_im = _s5(x, g[0], *params, s5_d[j], batch, h0)
                outs['s5re'][si].append(s_re.reshape(batch, *s5_a_re.shape[1:]))
                outs['s5im'][si].append(s_im.reshape(batch, *s5_a_re.shape[1:]))
                x, h = _glu_outproj(y, s5_w_glu[j].astype(BF16), s5_b_glu[j], x, g[1], g[2])
            else:
                heads, hd = cache_sb_k.shape[3], cache_sb_k.shape[4]
                past = cache_sb_k.shape[2]
                w_qkv = sb_w_qkv[j].astype(BF16)
                scale = hd ** -0.5
                q, = _linear(h, w_qkv, 0, heads * hd, (BF16,), (True,), name="sb_q")
                k32, k16 = _linear(h, w_qkv, heads * hd, heads * hd, (F32, BF16), (False, True), name="sb_k")
                v32, v16 = _linear(h, w_qkv, 2 * heads * hd, heads * hd, (F32, BF16), (False, True), name="sb_v")
                outs['sbk'][si].append(k32.reshape(batch, st['seq'], heads, hd))
                outs['sbv'][si].append(v32.reshape(batch, st['seq'], heads, hd))
                if si == 0:
                    o = _sb_flash(q, k16, v16, batch, scale)
                else:
                    o = _sb_sample(q, k16, v16, cache_sb_k, cache_sb_v, j, scale)
                x, h = _outproj(o, sb_w_o[j].astype(BF16), x, g[1], g[2], name="sb_outproj")
            x, h, conv = _conv_ffn(h, x, ffn_up, ffn_conv_w[i], ffn_conv_b[i], ffn_down, g[3], g_next, st['seq'],
                                   state=state_ffn_conv[i] if si == 1 else None, emit_h=emit_next)
            outs['conv'][si].append(conv)
            xs[si], hs[si] = x, h

    stack = lambda name: (jnp.stack(outs[name][0]), jnp.stack(outs[name][1]))
    return (xs[0].reshape(x_prompt.shape), xs[1].reshape(x_sample.shape),
            *stack('ret'), *stack('ckv'), *stack('kr'), *stack('s5re'), *stack('s5im'),
            *stack('sbk'), *stack('sbv'), *stack('conv'))
```

```python
import functools
import math

import jax
import jax.numpy as jnp
from jax import lax
from jax.experimental import pallas as pl
from jax.experimental.pallas import tpu as pltpu

F32 = jnp.float32
BF16 = jnp.bfloat16
NORM_EPS = 1e-6
GN_EPS = 1e-5
ROPE_BASE = 10000.0
CHUNK = 64
CONV_W = 3
LANES = 128
VMEM_LIMIT = 56 * 1024 * 1024
SB_EXIT_SUM = 104.0
FFN_COLS = 512


def _cparams(*sem):
    return pltpu.CompilerParams(dimension_semantics=sem, vmem_limit_bytes=VMEM_LIMIT)


def _tile(n, pref):
    t = min(n, pref)
    while n % t:
        t -= 1
    return t


def _dot(a, b):
    return jnp.dot(a, b, preferred_element_type=F32)


def _dot_nt(a, b):
    return lax.dot_general(a, b, (((1,), (1,)), ((), ())), preferred_element_type=F32)


def _rms(x, g):
    return x * lax.rsqrt(jnp.mean(x * x, axis=-1, keepdims=True) + NORM_EPS) * g


def _gelu_tanh(x):
    k = -2.0 * math.sqrt(2.0 / math.pi) * math.log2(math.e)
    return x / (1.0 + jnp.exp2(x * (k + (k * 0.044715) * (x * x))))


def _softplus(z):
    return jnp.maximum(z, 0.0) + jnp.log(1.0 + jnp.exp(-jnp.abs(z)))


def _rmsnorm_kernel(x_ref, g_ref, o_ref):
    o_ref[...] = _rms(x_ref[...], g_ref[...]).astype(o_ref.dtype)


def _rmsnorm(x, g, out_dtype=BF16, tm=512):
    m, d = x.shape
    tm = _tile(m, tm)
    return pl.pallas_call(
        _rmsnorm_kernel,
        grid=(m // tm,),
        in_specs=[pl.BlockSpec((tm, d), lambda i: (i, 0)), pl.BlockSpec((1, d), lambda i: (0, 0))],
        out_specs=pl.BlockSpec((tm, d), lambda i: (i, 0)),
        out_shape=jax.ShapeDtypeStruct((m, d), out_dtype),
        compiler_params=_cparams("parallel"),
        name="rmsnorm",
    )(x, g.reshape(1, d))


def _linear_kernel(a_ref, w_ref, *o_refs, head_major):
    acc = _dot(a_ref[...].astype(BF16), w_ref[...])
    for o_ref in o_refs:
        if head_major and len(o_ref.shape) == 3:
            for hh in range(o_ref.shape[0]):
                o_ref[hh] = acc[:, hh * LANES:(hh + 1) * LANES].astype(o_ref.dtype)
        else:
            o_ref[...] = acc.astype(o_ref.dtype)


def _linear(a, w, col_off, n_out, out_dtypes, head_major=(), tm=1024, tn=512, name="linear"):
    m, k = a.shape
    tm, tn = _tile(m, tm), _tile(n_out, tn)
    assert col_off % tn == 0 and tn % LANES == 0
    joff = col_off // tn
    out_specs, out_shape = [], []
    for dt, hm in zip(out_dtypes, head_major or (False,) * len(out_dtypes)):
        if hm:
            out_specs.append(pl.BlockSpec((tn // LANES, tm, LANES), lambda i, j: (j, i, 0)))
            out_shape.append(jax.ShapeDtypeStruct((n_out // LANES, m, LANES), dt))
        else:
            out_specs.append(pl.BlockSpec((tm, tn), lambda i, j: (i, j)))
            out_shape.append(jax.ShapeDtypeStruct((m, n_out), dt))
    return pl.pallas_call(
        functools.partial(_linear_kernel, head_major=bool(head_major)),
        grid=(m // tm, n_out // tn),
        in_specs=[pl.BlockSpec((tm, k), lambda i, j: (i, 0)),
                  pl.BlockSpec((k, tn), lambda i, j: (0, joff + j))],
        out_specs=out_specs,
        out_shape=out_shape,
        compiler_params=_cparams("parallel", "arbitrary"),
        name=name,
    )(a, w)


def _linear_rope_kernel(a_ref, w_ref, cos_ref, sin_ref, o_ref, *, dk, n_q_tiles, k_scale):
    acc = _dot(a_ref[...], w_ref[...])
    scale = jnp.where(pl.program_id(1) >= n_q_tiles, k_scale, 1.0).astype(F32)
    cos, sin = cos_ref[...] * scale, sin_ref[...] * scale
    half = dk // 2
    for hh in range(acc.shape[1] // dk):
        x1 = acc[:, hh * dk:hh * dk + half]
        x2 = acc[:, hh * dk + half:(hh + 1) * dk]
        o_ref[:, hh * dk:hh * dk + half] = x1 * cos - x2 * sin
        o_ref[:, hh * dk + half:(hh + 1) * dk] = x1 * sin + x2 * cos


def _linear_rope(a, w, n_out, cos, sin, dk, n_q, k_scale, tm=1024, tn=512):
    m, k = a.shape
    ltab = cos.shape[0]
    tm = _tile(math.gcd(m, ltab), tm)
    tn = _tile(n_out, tn)
    ntab = ltab // tm
    return pl.pallas_call(
        functools.partial(_linear_rope_kernel, dk=dk, n_q_tiles=n_q // tn, k_scale=k_scale),
        grid=(m // tm, n_out // tn),
        in_specs=[pl.BlockSpec((tm, k), lambda i, j: (i, 0)),
                  pl.BlockSpec((k, tn), lambda i, j: (0, j)),
                  pl.BlockSpec((tm, dk // 2), lambda i, j: (i % ntab, 0)),
                  pl.BlockSpec((tm, dk // 2), lambda i, j: (i % ntab, 0))],
        out_specs=pl.BlockSpec((tm, tn), lambda i, j: (i, j)),
        out_shape=jax.ShapeDtypeStruct((m, n_out), F32),
        compiler_params=_cparams("parallel", "arbitrary"),
        name="ret_qk_rope",
    )(a, w, cos, sin)


def _outproj_kernel(a_ref, w_ref, x_ref, gp_ref, gn_ref, xo_ref, *rest, emit_h):
    acc_ref = rest[-1]
    kk = pl.program_id(1)

    @pl.when(kk == 0)
    def _():
        acc_ref[...] = jnp.zeros_like(acc_ref)

    acc_ref[...] += _dot(a_ref[...].astype(BF16), w_ref[...])

    @pl.when(kk == pl.num_programs(1) - 1)
    def _():
        xn = x_ref[...] + _rms(acc_ref[...], gp_ref[...])
        xo_ref[...] = xn
        if emit_h:
            rest[0][...] = _rms(xn, gn_ref[...]).astype(BF16)


def _outproj(a, w, x, g_post, g_next, emit_h=True, tm=512, tk=512, name="outproj"):
    m, k = a.shape
    d = w.shape[1]
    tm, tk = _tile(m, tm), _tile(k, tk)
    out_specs = [pl.BlockSpec((tm, d), lambda i, kk: (i, 0))]
    out_shape = [jax.ShapeDtypeStruct((m, d), F32)]
    if emit_h:
        out_specs.append(pl.BlockSpec((tm, d), lambda i, kk: (i, 0)))
        out_shape.append(jax.ShapeDtypeStruct((m, d), BF16))
    res = pl.pallas_call(
        functools.partial(_outproj_kernel, emit_h=emit_h),
        grid=(m // tm, k // tk),
        in_specs=[pl.BlockSpec((tm, tk), lambda i, kk: (i, kk)),
                  pl.BlockSpec((tk, d), lambda i, kk: (kk, 0)),
                  pl.BlockSpec((tm, d), lambda i, kk: (i, 0)),
                  pl.BlockSpec((1, d), lambda i, kk: (0, 0)),
                  pl.BlockSpec((1, d), lambda i, kk: (0, 0))],
        out_specs=out_specs,
        out_shape=out_shape,
        scratch_shapes=[pltpu.VMEM((tm, d), F32)],
        compiler_params=_cparams("parallel", "arbitrary"),
        name=name,
    )(a, w, x, g_post.reshape(1, d), g_next.reshape(1, d))
    return (res[0], res[1]) if emit_h else (res[0], None)


def _glu_outproj_kernel(a_ref, wv_ref, wg_ref, bv_ref, bg_ref, x_ref, gp_ref, gn_ref, xo_ref, ho_ref,
                        accv_ref, accg_ref):
    kk = pl.program_id(1)

    @pl.when(kk == 0)
    def _():
        accv_ref[...] = jnp.zeros_like(accv_ref)
        accg_ref[...] = jnp.zeros_like(accg_ref)

    a = a_ref[...]
    accv_ref[...] += _dot(a, wv_ref[...])
    accg_ref[...] += _dot(a, wg_ref[...])

    @pl.when(kk == pl.num_programs(1) - 1)
    def _():
        mix = (accv_ref[...] + bv_ref[...]) * jax.nn.sigmoid(accg_ref[...] + bg_ref[...])
        xn = x_ref[...] + _rms(mix, gp_ref[...])
        xo_ref[...] = xn
        ho_ref[...] = _rms(xn, gn_ref[...]).astype(BF16)


def _glu_outproj(a, w, b, x, g_post, g_next, tm=512, tk=512):
    m, k = a.shape
    d = w.shape[1] // 2
    tm, tk = _tile(m, tm), _tile(k, tk)
    row = lambda i, kk: (i, 0)
    const = lambda i, kk: (0, 0)
    return pl.pallas_call(
        _glu_outproj_kernel,
        grid=(m // tm, k // tk),
        in_specs=[pl.BlockSpec((tm, tk), lambda i, kk: (i, kk)),
                  pl.BlockSpec((tk, d), lambda i, kk: (kk, 0)),
                  pl.BlockSpec((tk, d), lambda i, kk: (kk, 1)),
                  pl.BlockSpec((1, d), lambda i, kk: (0, 0)),
                  pl.BlockSpec((1, d), lambda i, kk: (0, 1)),
                  pl.BlockSpec((tm, d), row), pl.BlockSpec((1, d), const), pl.BlockSpec((1, d), const)],
        out_specs=[pl.BlockSpec((tm, d), row), pl.BlockSpec((tm, d), row)],
        out_shape=[jax.ShapeDtypeStruct((m, d), F32), jax.ShapeDtypeStruct((m, d), BF16)],
        scratch_shapes=[pltpu.VMEM((tm, d), F32), pltpu.VMEM((tm, d), F32)],
        compiler_params=_cparams("parallel", "arbitrary"),
        name="s5_glu_outproj",
    )(a, w, w, b.reshape(1, 2 * d), b.reshape(1, 2 * d), x, g_post.reshape(1, d), g_next.reshape(1, d))


def _ffn_kernel(*refs, n_seq, has_state, emit_h, tiles_per_seq, nf):
    (h_ref, x_ref, wv_ref, wg_ref, cwv_ref, cwg_ref, cbv_ref, cbg_ref, wd_ref, gp_ref, gn_ref), refs = refs[:11], refs[11:]
    if has_state:
        (stv_ref, stg_ref), refs = refs[:2], refs[2:]
    xo_ref, refs = refs[0], refs[1:]
    if emit_h:
        ho_ref, refs = refs[0], refs[1:]
    cv_ref, cg_ref, acc_ref, act_even_ref, act_odd_ref = refs[:5]
    carry_refs = refs[5:]
    i, j = pl.program_id(0), pl.program_id(1)
    tm, tf = h_ref.shape[0], wv_ref.shape[1]
    seq = tm // n_seq

    if not has_state:
        @pl.when((i == 0) & (j == 0))
        def _():
            for carry_ref in carry_refs[:2]:
                carry_ref[...] = jnp.zeros_like(carry_ref)

    def conv_with_state(w_ref, cw_ref, cb_ref, st_ref, c_out_ref):
        up = _dot(h_ref[...], w_ref[...]).reshape(n_seq, seq, tf)
        prev1, prev2 = st_ref[:, 1:2, :], st_ref[:, 0:1, :]
        pos = lax.broadcasted_iota(jnp.int32, up.shape, 1)
        p1 = jnp.where(pos == 0, prev1, pltpu.roll(up, 1, axis=1))
        p2 = jnp.where(pos == 0, prev2, jnp.where(pos == 1, prev1, pltpu.roll(up, 2, axis=1)))
        c_out_ref[...] = up[:, seq - 2:, :]
        cw = cw_ref[...]
        c = cb_ref[...][None] + cw[0:1][None] * p2 + cw[1:2][None] * p1 + cw[2:3][None] * up
        return c.reshape(tm, tf)

    def conv_with_carry(w_ref, cw_ref, cb_ref, carry_ref, ext_ref, c_out_ref):
        up = _dot(h_ref[...], w_ref[...])
        ext_ref[0:8, :] = jnp.where(i % tiles_per_seq != 0, carry_ref[j], 0.0)
        ext_ref[8:, :] = up
        carry_ref[j] = up[tm - 8:, :]
        c_out_ref[0] = up[tm - 2:, :]
        cw = cw_ref[...]
        return cb_ref[...] + cw[0:1] * ext_ref[6:6 + tm, :] + cw[1:2] * ext_ref[7:7 + tm, :] + cw[2:3] * up

    def activation():
        if has_state:
            val = conv_with_state(wv_ref, cwv_ref, cbv_ref, stv_ref, cv_ref)
            gate = conv_with_state(wg_ref, cwg_ref, cbg_ref, stg_ref, cg_ref)
        else:
            val = conv_with_carry(wv_ref, cwv_ref, cbv_ref, carry_refs[0], carry_refs[2], cv_ref)
            gate = conv_with_carry(wg_ref, cwg_ref, cbg_ref, carry_refs[1], carry_refs[3], cg_ref)
        return (_gelu_tanh(gate) * val).astype(BF16)

    @pl.when(j == 0)
    def _():
        acc_ref[...] = jnp.zeros_like(acc_ref)
        act_odd_ref[...] = jnp.zeros_like(act_odd_ref)

    def step(act_new_ref, act_old_ref):
        act_new_ref[...] = activation()
        acc_ref[...] += _dot(act_old_ref[...], wd_ref[...])

    @pl.when((j < nf) & (j % 2 == 0))
    def _():
        step(act_even_ref, act_odd_ref)

    @pl.when((j < nf) & (j % 2 == 1))
    def _():
        step(act_odd_ref, act_even_ref)

    @pl.when(j == nf)
    def _():
        last = act_even_ref if (nf - 1) % 2 == 0 else act_odd_ref
        acc_ref[...] += _dot(last[...], wd_ref[...])
        xn = x_ref[...] + _rms(acc_ref[...], gp_ref[...])
        xo_ref[...] = xn
        if emit_h:
            ho_ref[...] = _rms(xn, gn_ref[...]).astype(BF16)


def _conv_ffn(h, x, w_up, conv_w, conv_b, w_down, g_post, g_next, seq_len, state=None, emit_h=True, tm=512):
    m, d = h.shape
    d_ff = w_down.shape[0]
    tf = _tile(d_ff, FFN_COLS)
    nf = d_ff // tf
    has_state = state is not None
    if has_state:
        tm = _tile(m // seq_len, max(tm // seq_len, 1)) * seq_len
        n_seq, tiles_per_seq = tm // seq_len, 1
    else:
        tm = _tile(seq_len, tm)
        n_seq, tiles_per_seq = 1, seq_len // tm
    n_seqs = m // seq_len
    row = lambda i, j: (i, 0)
    const = lambda i, j: (0, 0)
    up = lambda j: jnp.minimum(j, nf - 1)
    val_cols = lambda i, j: (0, up(j))
    gate_cols = lambda i, j: (0, nf + up(j))
    in_specs = [pl.BlockSpec((tm, d), row), pl.BlockSpec((tm, d), row),
                pl.BlockSpec((d, tf), val_cols), pl.BlockSpec((d, tf), gate_cols),
                pl.BlockSpec((CONV_W, tf), val_cols), pl.BlockSpec((CONV_W, tf), gate_cols),
                pl.BlockSpec((1, tf), val_cols), pl.BlockSpec((1, tf), gate_cols),
                pl.BlockSpec((tf, d), lambda i, j: (jnp.maximum(j - 1, 0), 0)),
                pl.BlockSpec((1, d), const), pl.BlockSpec((1, d), const)]
    args = [h, x, w_up, w_up, conv_w, conv_w, conv_b.reshape(1, -1), conv_b.reshape(1, -1), w_down,
            g_post.reshape(1, d), g_next.reshape(1, d)]
    if has_state:
        in_specs += [pl.BlockSpec((n_seq, CONV_W - 1, tf), lambda i, j: (i, 0, up(j))),
                     pl.BlockSpec((n_seq, CONV_W - 1, tf), lambda i, j: (i, 0, nf + up(j)))]
        args += [state, state]
        scratch = []
    else:
        scratch = [pltpu.VMEM((nf, 8, tf), F32), pltpu.VMEM((nf, 8, tf), F32),
                   pltpu.VMEM((tm + 8, tf), F32), pltpu.VMEM((tm + 8, tf), F32)]
    conv_map = lambda i, j: (i, 0, up(j))
    out_specs = [pl.BlockSpec((tm, d), row)]
    out_shape = [jax.ShapeDtypeStruct((m, d), F32)]
    if emit_h:
        out_specs.append(pl.BlockSpec((tm, d), row))
        out_shape.append(jax.ShapeDtypeStruct((m, d), BF16))
    out_specs += [pl.BlockSpec((n_seq, CONV_W - 1, tf), conv_map)] * 2
    out_shape += [jax.ShapeDtypeStruct((n_seqs * tiles_per_seq, CONV_W - 1, d_ff), F32)] * 2
    res = pl.pallas_call(
        functools.partial(_ffn_kernel, n_seq=n_seq, has_state=has_state, emit_h=emit_h, tiles_per_seq=tiles_per_seq,
                          nf=nf),
        grid=(m // tm, nf + 1),
        in_specs=in_specs,
        out_specs=out_specs,
        out_shape=out_shape,
        scratch_shapes=[pltpu.VMEM((tm, d), F32), pltpu.VMEM((tm, tf), BF16), pltpu.VMEM((tm, tf), BF16)] + scratch,
        compiler_params=_cparams("arbitrary", "arbitrary"),
        name="conv_ffn_state" if has_state else "conv_ffn",
    )(*args)
    x_new = res[0]
    h_new = res[1] if emit_h else None
    conv_state = jnp.concatenate(res[-2:], axis=-1)[tiles_per_seq - 1::tiles_per_seq]
    return x_new, h_new, conv_state


def _ret_kernel(*refs, has_s0):
    if has_s0:
        lg_ref, q_ref, k_ref, v_ref, g_ref, s0_ref, y_ref, so_ref, s_ref = refs
    else:
        lg_ref, q_ref, k_ref, v_ref, g_ref, y_ref, so_ref, s_ref = refs
    c = pl.program_id(2)
    cl = q_ref.shape[0]

    @pl.when(c == 0)
    def _():
        s_ref[...] = s0_ref[0, 0] if has_s0 else jnp.zeros_like(s_ref)

    lg = lg_ref[0, 0:1, 0:1]
    q, k, v = q_ref[...], k_ref[...], v_ref[...]
    row = lax.broadcasted_iota(jnp.int32, (cl, cl), 0)
    col = lax.broadcasted_iota(jnp.int32, (cl, cl), 1)
    diff = (row - col).astype(F32)
    decay = jnp.where(diff >= 0, jnp.exp(jnp.maximum(diff, 0.0) * lg), 0.0)
    scores = _dot_nt(q.astype(BF16), k.astype(BF16)) * decay
    o = _dot(scores.astype(BF16), v)
    idx = lax.broadcasted_iota(jnp.int32, (cl, 1), 0).astype(F32)
    s_prev = s_ref[...]
    o = o + _dot((q * jnp.exp((idx + 1.0) * lg)).astype(BF16), s_prev.astype(BF16))
    k_dec = k * jnp.exp((cl - 1.0 - idx) * lg)
    s_new = jnp.exp(cl * lg) * s_prev + _dot(k_dec.T.astype(BF16), v)
    s_ref[...] = s_new
    mu = jnp.mean(o, axis=-1, keepdims=True)
    var = jnp.mean(jnp.square(o - mu), axis=-1, keepdims=True)
    of = (o - mu) * lax.rsqrt(var + GN_EPS)
    g = g_ref[...]
    y_ref[...] = (g * jax.nn.sigmoid(g) * of).astype(y_ref.dtype)

    @pl.when(c == pl.num_programs(2) - 1)
    def _():
        so_ref[0, 0] = s_new


def _retention(qk, v, g, batch, heads, s0=None, chunk=256):
    m = qk.shape[0]
    dk, dv = qk.shape[1] // (2 * heads), v.shape[1] // heads
    seq = m // batch
    cl = _tile(seq, chunk)
    nc = seq // cl
    lg = jnp.log1p(-jnp.exp2(-5.0 - jnp.arange(heads, dtype=F32)))
    lg_tab = jnp.broadcast_to(lg[:, None, None], (heads, 8, LANES))
    tok = lambda b, h, c: (b * nc + c, h)
    in_specs = [pl.BlockSpec((1, 8, LANES), lambda b, h, c: (h, 0, 0)),
                pl.BlockSpec((cl, dk), tok), pl.BlockSpec((cl, dk), lambda b, h, c: (b * nc + c, heads + h)),
                pl.BlockSpec((cl, dv), tok), pl.BlockSpec((cl, dv), tok)]
    args = [lg_tab, qk, qk, v, g]
    if s0 is not None:
        in_specs.append(pl.BlockSpec((1, 1, dk, dv), lambda b, h, c: (b, h, 0, 0)))
        args.append(s0)
    return pl.pallas_call(
        functools.partial(_ret_kernel, has_s0=s0 is not None),
        grid=(batch, heads, nc),
        in_specs=in_specs,
        out_specs=[pl.BlockSpec((cl, dv), tok), pl.BlockSpec((1, 1, dk, dv), lambda b, h, c: (b, h, 0, 0))],
        out_shape=[jax.ShapeDtypeStruct((m, heads * dv), BF16), jax.ShapeDtypeStruct((batch, heads, dk, dv), F32)],
        scratch_shapes=[pltpu.VMEM((dk, dv), F32)],
        compiler_params=_cparams("parallel", "parallel", "arbitrary"),
        name="retention",
    )(*args)


def _mla_down_kernel(h_ref, wq_ref, wkv_ref, wr_ref, wrr_ref, gq_ref, gkv_ref, cos_ref, sin_ref,
                     cq_ref, ckv_ref, kr_ref):
    h = h_ref[...]
    cq_ref[...] = _rms(_dot(h, wq_ref[...]), gq_ref[...]).astype(cq_ref.dtype)
    ckv_ref[...] = _rms(_dot(h, wkv_ref[...]), gkv_ref[...])
    kr_ref[...] = _dot(h, wr_ref[...]) * cos_ref[...] + _dot(h, wrr_ref[...]) * sin_ref[...]


def _mla_down(h, wq, wkv, wr, wrr, gq, gkv, cos, sin, tm=512):
    m, d = h.shape
    ql, kvl, rd = wq.shape[1], wkv.shape[1], wr.shape[1]
    ltab = cos.shape[0]
    tm = _tile(math.gcd(m, ltab), tm)
    ntab = ltab // tm
    const = lambda i: (0, 0)
    row = lambda i: (i, 0)
    tab = lambda i: (i % ntab, 0)
    return pl.pallas_call(
        _mla_down_kernel,
        grid=(m // tm,),
        in_specs=[pl.BlockSpec((tm, d), row), pl.BlockSpec((d, ql), const), pl.BlockSpec((d, kvl), const),
                  pl.BlockSpec((d, rd), const), pl.BlockSpec((d, rd), const),
                  pl.BlockSpec((1, ql), const), pl.BlockSpec((1, kvl), const),
                  pl.BlockSpec((tm, rd), tab), pl.BlockSpec((tm, rd), tab)],
        out_specs=[pl.BlockSpec((tm, ql), row), pl.BlockSpec((tm, kvl), row), pl.BlockSpec((tm, rd), row)],
        out_shape=[jax.ShapeDtypeStruct((m, ql), BF16), jax.ShapeDtypeStruct((m, kvl), F32),
                   jax.ShapeDtypeStruct((m, rd), F32)],
        compiler_params=_cparams("parallel"),
        name="mla_down",
    )(h, wq, wkv, wr, wrr, gq.reshape(1, ql), gkv.reshape(1, kvl), cos, sin)


def _mla_q_kernel(cq_ref, wn_ref, wr_ref, wrr_ref, cos_ref, sin_ref, q_ref, *, scale):
    cq = cq_ref[...]
    qn = _dot(cq, wn_ref[...]) * scale
    nope = LANES
    for hh in range(q_ref.shape[0]):
        q_ref[hh, :, :nope] = qn[:, hh * LANES:(hh + 1) * LANES].astype(q_ref.dtype)
    a, b = _dot(cq, wr_ref[...]), _dot(cq, wrr_ref[...])
    cos, sin = cos_ref[...] * scale, sin_ref[...] * scale
    rd = q_ref.shape[2] - nope
    per = LANES // rd
    for p in range(a.shape[1] // LANES):
        r = a[:, p * LANES:(p + 1) * LANES] * cos + b[:, p * LANES:(p + 1) * LANES] * sin
        for s in range(per):
            q_ref[p * per + s, :, nope:] = r[:, s * rd:(s + 1) * rd].astype(q_ref.dtype)


def _mla_q(cq, wn, wr, wrr, cos, sin, heads, scale, tm=512):
    m, ql = cq.shape
    nope, rd = wn.shape[1] // heads, wr.shape[1] // heads
    assert nope == LANES and LANES % rd == 0
    ltab = cos.shape[0]
    tm = _tile(math.gcd(m, ltab), tm)
    ntab = ltab // tm
    const = lambda i: (0, 0)
    tab = lambda i: (i % ntab, 0)
    return pl.pallas_call(
        functools.partial(_mla_q_kernel, scale=scale),
        grid=(m // tm,),
        in_specs=[pl.BlockSpec((tm, ql), lambda i: (i, 0)), pl.BlockSpec(wn.shape, const),
                  pl.BlockSpec(wr.shape, const), pl.BlockSpec(wrr.shape, const),
                  pl.BlockSpec((tm, LANES), tab), pl.BlockSpec((tm, LANES), tab)],
        out_specs=pl.BlockSpec((heads, tm, nope + rd), lambda i: (0, i, 0)),
        out_shape=jax.ShapeDtypeStruct((heads, m, nope + rd), BF16),
        compiler_params=_cparams("parallel"),
        name="mla_q",
    )(cq, wn, wr, wrr, cos, sin)


def _mla_kv_kernel(ckv_ref, kr_ref, wk_ref, wv_ref, k_ref, v_ref):
    ckv = ckv_ref[...].astype(BF16)
    kn, v = _dot(ckv, wk_ref[...]), _dot(ckv, wv_ref[...])
    kr = kr_ref[...].astype(k_ref.dtype)
    nope = k_ref.shape[2] - kr.shape[1]
    vd = v_ref.shape[2]
    for hh in range(k_ref.shape[0]):
        k_ref[hh, :, :nope] = kn[:, hh * nope:(hh + 1) * nope].astype(k_ref.dtype)
        k_ref[hh, :, nope:] = kr
        v_ref[hh] = v[:, hh * vd:(hh + 1) * vd].astype(v_ref.dtype)


def _mla_kv(ckv, kr, wk, wv, heads, tm=512):
    m, kvl = ckv.shape
    rd = kr.shape[1]
    nope, vd = wk.shape[1] // heads, wv.shape[1] // heads
    tm = _tile(m, tm)
    const = lambda i: (0, 0)
    return pl.pallas_call(
        _mla_kv_kernel,
        grid=(m // tm,),
        in_specs=[pl.BlockSpec((tm, kvl), lambda i: (i, 0)), pl.BlockSpec((tm, rd), lambda i: (i, 0)),
                  pl.BlockSpec(wk.shape, const), pl.BlockSpec(wv.shape, const)],
        out_specs=[pl.BlockSpec((heads, tm, nope + rd), lambda i: (0, i, 0)),
                   pl.BlockSpec((heads, tm, vd), lambda i: (0, i, 0))],
        out_shape=[jax.ShapeDtypeStruct((heads, m, nope + rd), BF16), jax.ShapeDtypeStruct((heads, m, vd), BF16)],
        compiler_params=_cparams("parallel"),
        name="mla_kv",
    )(ckv, kr, wk, wv)


def _mla_flash_kernel(q_ref, k_ref, v_ref, o_ref, m_ref, l_ref, acc_ref):
    qi, ki = pl.program_id(1), pl.program_id(2)
    nh, tq, vd = acc_ref.shape
    tk = k_ref.shape[1]

    @pl.when(ki == 0)
    def _():
        m_ref[...] = jnp.full_like(m_ref, -jnp.inf)
        l_ref[...] = jnp.zeros_like(l_ref)
        acc_ref[...] = jnp.zeros_like(acc_ref)

    def sweep(diagonal):
        if diagonal:
            row_chunk = lax.broadcasted_iota(jnp.int32, (tq, tk), 0) // CHUNK
            col_chunk = lax.broadcasted_iota(jnp.int32, (tq, tk), 1) // CHUNK
            mask = col_chunk <= row_chunk

        def head(hh, carry):
            s = _dot_nt(q_ref[hh], k_ref[hh])
            if diagonal:
                s = jnp.where(mask, s, -jnp.inf)
            m_prev = m_ref[hh]
            m_new = jnp.maximum(m_prev, jnp.max(s, axis=-1, keepdims=True))
            alpha = jnp.exp(m_prev - m_new)
            p = jnp.concatenate([jnp.exp(s[:, c * LANES:(c + 1) * LANES] - m_new) for c in range(tk // LANES)],
                                axis=-1)
            l_ref[hh] = alpha * l_ref[hh] + jnp.sum(p, axis=-1, keepdims=True)
            acc_ref[hh] = alpha * acc_ref[hh] + _dot(p.astype(BF16), v_ref[hh])
            m_ref[hh] = m_new
            return carry

        lax.fori_loop(0, nh, head, 0, unroll=8)

    @pl.when(ki < qi)
    def _():
        sweep(False)

    @pl.when(ki == qi)
    def _():
        sweep(True)
        for hh in range(nh):
            o_ref[:, hh * vd:(hh + 1) * vd] = (acc_ref[hh] / l_ref[hh]).astype(o_ref.dtype)


def _mla_flash(q, k, v, batch, t=512):
    heads, m, qd = q.shape
    vd = v.shape[2]
    seq = m // batch
    t = _tile(seq, t)
    assert t % CHUNK == 0 and vd == LANES
    nq = seq // t
    qmap = lambda b, qi, ki: (0, b * nq + qi, 0)
    kmap = lambda b, qi, ki: (0, b * nq + jnp.minimum(ki, qi), 0)
    return pl.pallas_call(
        _mla_flash_kernel,
        grid=(batch, nq, nq),
        in_specs=[pl.BlockSpec((heads, t, qd), qmap), pl.BlockSpec((heads, t, qd), kmap),
                  pl.BlockSpec((heads, t, vd), kmap)],
        out_specs=pl.BlockSpec((t, heads * vd), lambda b, qi, ki: (b * nq + qi, 0)),
        out_shape=jax.ShapeDtypeStruct((m, heads * vd), BF16),
        scratch_shapes=[pltpu.VMEM((heads, t, LANES), F32), pltpu.VMEM((heads, t, LANES), F32),
                        pltpu.VMEM((heads, t, vd), F32)],
        compiler_params=_cparams("parallel", "parallel", "arbitrary"),
        name="mla_flash",
    )(q, k, v)


def _mla_sample_kernel(q_ref, ckv_ref, kr_ref, cckv_ref, ckr_ref, wuk_ref, wuv_ref, o_ref, ql_ref):
    nh, t, qd = q_ref.shape
    kvl = ckv_ref.shape[1]
    nope, vd = wuk_ref.shape[2], wuv_ref.shape[2]
    for hh in range(nh):
        ql_ref[hh] = _dot_nt(q_ref[hh, :, :nope], wuk_ref[hh]).astype(BF16)
    q_lat = ql_ref[...].reshape(nh * t, kvl)
    q_rope = q_ref[:, :, nope:].reshape(nh * t, qd - nope)
    c_old, r_old = cckv_ref[0].astype(BF16), ckr_ref[0].astype(BF16)
    c_new, r_new = ckv_ref[...].astype(BF16), kr_ref[...].astype(BF16)
    s_old = _dot_nt(q_lat, c_old) + _dot_nt(q_rope, r_old)
    s_new = _dot_nt(q_lat, c_new) + _dot_nt(q_rope, r_new)
    mx = jnp.maximum(jnp.max(s_old, axis=-1, keepdims=True), jnp.max(s_new, axis=-1, keepdims=True))
    p_old, p_new = jnp.exp(s_old - mx), jnp.exp(s_new - mx)
    denom = jnp.sum(p_old, axis=-1, keepdims=True) + jnp.sum(p_new, axis=-1, keepdims=True)
    o_lat = (_dot(p_old.astype(BF16), c_old) + _dot(p_new.astype(BF16), c_new)) / denom
    o_lat = o_lat.astype(BF16)
    for hh in range(nh):
        o_ref[:, hh * vd:(hh + 1) * vd] = _dot(o_lat[hh * t:(hh + 1) * t], wuv_ref[hh]).astype(o_ref.dtype)


def _mla_sample(q, ckv, kr, cache_ckv, cache_kr, wuk_t, wuv_t):
    heads, m, qd = q.shape
    batch, past, kvl = cache_ckv.shape
    t = m // batch
    rd = kr.shape[1]
    vd = wuv_t.shape[2]
    const3 = lambda b: (0, 0, 0)
    return pl.pallas_call(
        _mla_sample_kernel,
        grid=(batch,),
        in_specs=[pl.BlockSpec((heads, t, qd), lambda b: (0, b, 0)),
                  pl.BlockSpec((t, kvl), lambda b: (b, 0)), pl.BlockSpec((t, rd), lambda b: (b, 0)),
                  pl.BlockSpec((1, past, kvl), lambda b: (b, 0, 0)), pl.BlockSpec((1, past, rd), lambda b: (b, 0, 0)),
                  pl.BlockSpec(wuk_t.shape, const3), pl.BlockSpec(wuv_t.shape, const3)],
        out_specs=pl.BlockSpec((t, heads * vd), lambda b: (b, 0)),
        out_shape=jax.ShapeDtypeStruct((m, heads * vd), BF16),
        scratch_shapes=[pltpu.VMEM((heads, t, kvl), BF16)],
        compiler_params=_cparams("parallel"),
        name="mla_sample",
    )(q, ckv, kr, cache_ckv, cache_kr, wuk_t, wuv_t)


def _s5_kernel(*refs, has_h0, col_block):
    (x_ref, g_ref, wbr_ref, wbi_ref, ar_ref, ai_ref, cr_ref, ci_ref, d_ref), refs = refs[:9], refs[9:]
    if has_h0:
        (h0r_ref, h0i_ref), refs = refs[:2], refs[2:]
    y_ref, sr_ref, si_ref, br_ref, bi_ref, hr_ref, hi_ref = refs
    tt = pl.program_id(1)
    t, d = x_ref.shape
    n_state = br_ref.shape[1]
    n_gb, gin, gst = wbr_ref.shape

    @pl.when(tt == 0)
    def _():
        if has_h0:
            hr_ref[...] = h0r_ref[0]
            hi_ref[...] = h0i_ref[0]
        else:
            hr_ref[...] = jnp.zeros_like(hr_ref)
            hi_ref[...] = jnp.zeros_like(hi_ref)

    h = _rms(x_ref[...], g_ref[...])
    ub = h.astype(BF16)
    for gb in range(n_gb):
        u_blk = ub[:, gb * gin:(gb + 1) * gin]
        br_ref[:, gb * gst:(gb + 1) * gst] = _dot(u_blk, wbr_ref[gb])
        bi_ref[:, gb * gst:(gb + 1) * gst] = _dot(u_blk, wbi_ref[gb])

    for cb in range(n_state // col_block):
        sl = slice(cb * col_block, (cb + 1) * col_block)
        a_re, a_im = ar_ref[:, sl], ai_ref[:, sl]

        def step(s, carry):
            h_re, h_im = carry
            n_re = a_re * h_re - a_im * h_im + br_ref[pl.ds(s, 1), sl]
            n_im = a_re * h_im + a_im * h_re + bi_ref[pl.ds(s, 1), sl]
            br_ref[pl.ds(s, 1), sl] = n_re
            bi_ref[pl.ds(s, 1), sl] = n_im
            return n_re, n_im

        h_re, h_im = lax.fori_loop(0, t, step, (hr_ref[:, sl], hi_ref[:, sl]))
        hr_ref[:, sl] = h_re
        hi_ref[:, sl] = h_im

    for gb in range(n_gb):
        hs_re = br_ref[:, gb * gst:(gb + 1) * gst].astype(BF16)
        hs_im = bi_ref[:, gb * gst:(gb + 1) * gst].astype(BF16)
        cs = slice(gb * gin, (gb + 1) * gin)
        y = _dot(hs_re, cr_ref[gb]) - _dot(hs_im, ci_ref[gb]) + d_ref[:, cs] * h[:, cs]
        y_ref[:, cs] = _gelu_tanh(y).astype(y_ref.dtype)

    @pl.when(tt == pl.num_programs(1) - 1)
    def _():
        sr_ref[0] = hr_ref[...]
        si_ref[0] = hi_ref[...]


def _s5(x, g, wb_re, wb_im, a_re, a_im, wc_re, wc_im, d_skip, batch, h0=None, t=256, col_block=4096):
    m, d = x.shape
    seq = m // batch
    t = _tile(seq, t)
    nt = seq // t
    n_state = a_re.shape[1]
    col_block = _tile(n_state, col_block)
    const2 = lambda b, tt: (0, 0)
    const3 = lambda b, tt: (0, 0, 0)
    in_specs = [pl.BlockSpec((t, d), lambda b, tt: (b * nt + tt, 0)), pl.BlockSpec((1, d), const2),
                pl.BlockSpec(wb_re.shape, const3), pl.BlockSpec(wb_im.shape, const3),
                pl.BlockSpec((1, n_state), const2), pl.BlockSpec((1, n_state), const2),
                pl.BlockSpec(wc_re.shape, const3), pl.BlockSpec(wc_im.shape, const3),
                pl.BlockSpec((1, d), const2)]
    args = [x, g.reshape(1, d), wb_re, wb_im, a_re, a_im, wc_re, wc_im, d_skip.reshape(1, d)]
    if h0 is not None:
        in_specs += [pl.BlockSpec((1, 1, n_state), lambda b, tt: (b, 0, 0))] * 2
        args += [h0[0].reshape(batch, 1, n_state), h0[1].reshape(batch, 1, n_state)]
    state_spec = pl.BlockSpec((1, 1, n_state), lambda b, tt: (b, 0, 0))
    return pl.pallas_call(
        functools.partial(_s5_kernel, has_h0=h0 is not None, col_block=col_block),
        grid=(batch, nt),
        in_specs=in_specs,
        out_specs=[pl.BlockSpec((t, d), lambda b, tt: (b * nt + tt, 0)), state_spec, state_spec],
        out_shape=[jax.ShapeDtypeStruct((m, d), BF16), jax.ShapeDtypeStruct((batch, 1, n_state), F32),
                   jax.ShapeDtypeStruct((batch, 1, n_state), F32)],
        scratch_shapes=[pltpu.VMEM((t, n_state), F32), pltpu.VMEM((t, n_state), F32),
                        pltpu.VMEM((1, n_state), F32), pltpu.VMEM((1, n_state), F32)],
        compiler_params=_cparams("parallel", "arbitrary"),
        name="s5_scan",
    )(*args)


def _s5_params(a_re, a_im, log_dt, b_re, b_im, c_re, c_im, groups_per_block=8):
    g, p, gc = b_re.shape
    step = jnp.exp(log_dt)[:, None]
    xr, xi = a_re * step, a_im * step
    ab_re, ab_im = jnp.exp(xr) * jnp.cos(xi), jnp.exp(xr) * jnp.sin(xi)
    m1_re = jnp.expm1(xr) * jnp.cos(xi) - 2.0 * jnp.square(jnp.sin(0.5 * xi))
    den = a_re * a_re + a_im * a_im
    f_re = (m1_re * a_re + ab_im * a_im) / den
    f_im = (ab_im * a_re - m1_re * a_im) / den
    bb_re = f_re[..., None] * b_re - f_im[..., None] * b_im
    bb_im = f_re[..., None] * b_im + f_im[..., None] * b_re
    gpb = groups_per_block
    eye = jnp.eye(gpb, dtype=F32)

    def blockdiag_b(w):
        w = w.reshape(g // gpb, gpb, p, gc)
        return jnp.einsum('bgpc,gh->bgchp', w, eye).reshape(g // gpb, gpb * gc, gpb * p).astype(BF16)

    def blockdiag_c(w):
        w = w.reshape(g // gpb, gpb, gc, p)
        return jnp.einsum('bgcp,gh->bgphc', w, eye).reshape(g // gpb, gpb * p, gpb * gc).astype(BF16)

    return (blockdiag_b(bb_re), blockdiag_b(bb_im), ab_re.reshape(1, g * p), ab_im.reshape(1, g * p),
            blockdiag_c(c_re), blockdiag_c(c_im))


def _sb_block(q, k, v, mask, tri, r_prev, scale):
    z = _dot_nt(q, k) * scale
    sp = _softplus(z)
    if mask is not None:
        sp = jnp.where(mask, sp, 0.0)
    hi = sp.astype(BF16)
    lo = (sp - hi.astype(F32)).astype(BF16)
    csum = _dot(hi, tri) + _dot(lo, tri)
    log_w = z - csum
    if mask is not None:
        log_w = jnp.where(mask, log_w, -jnp.inf)
    n_keys = z.shape[1]
    if n_keys % LANES:
        w = jnp.exp(log_w - r_prev[:, :n_keys])
    else:
        w = jnp.concatenate([jnp.exp(log_w[:, c * LANES:(c + 1) * LANES] - r_prev) for c in range(n_keys // LANES)],
                            axis=-1)
    return _dot(w.astype(BF16), v), r_prev + jnp.sum(sp, axis=-1, keepdims=True)


def _tri(n):
    row = lax.broadcasted_iota(jnp.int32, (n, n), 0)
    col = lax.broadcasted_iota(jnp.int32, (n, n), 1)
    return (row >= col).astype(BF16)


def _sb_flash_kernel(q_ref, k_hbm, v_hbm, o_ref, kbuf, vbuf, sem, r_ref, acc_ref, *, scale, seq):
    b, qi = pl.program_id(0), pl.program_id(1)
    nh, t, hd = acc_ref.shape
    row0 = b * seq + qi * t

    def copies(kj, slot):
        start = pl.multiple_of(row0 - kj * t, t)
        return (pltpu.make_async_copy(k_hbm.at[:, pl.ds(start, t), :], kbuf.at[slot], sem.at[0, slot]),
                pltpu.make_async_copy(v_hbm.at[:, pl.ds(start, t), :], vbuf.at[slot], sem.at[1, slot]))

    for cp in copies(0, 0):
        cp.start()
    r_ref[...] = jnp.zeros_like(r_ref)
    acc_ref[...] = jnp.zeros_like(acc_ref)
    row = lax.broadcasted_iota(jnp.int32, (t, t), 0)
    col = lax.broadcasted_iota(jnp.int32, (t, t), 1)
    tri = _tri(t)

    def sweep(carry):
        kj, _ = carry
        slot = kj % 2
        for cp in copies(kj, slot):
            cp.wait()

        @pl.when(kj < qi)
        def _():
            for cp in copies(kj + 1, 1 - slot):
                cp.start()

        mask = (col < row) | (kj > 0)

        def head(hh, r_min):
            o, r = _sb_block(q_ref[hh], kbuf[slot, hh], vbuf[slot, hh], mask, tri, r_ref[hh], scale)
            acc_ref[hh] += o
            r_ref[hh] = r
            return jnp.minimum(r_min, jnp.min(r))

        r_min = lax.fori_loop(0, nh, head, jnp.float32(jnp.inf), unroll=8)
        return kj + 1, (r_min > SB_EXIT_SUM).astype(jnp.int32)

    kj_end, _ = lax.while_loop(lambda c: (c[0] <= qi) & (c[1] == 0), sweep, (jnp.int32(0), jnp.int32(0)))

    @pl.when(kj_end <= qi)
    def _():
        for cp in copies(kj_end, kj_end % 2):
            cp.wait()

    for hh in range(nh):
        o_ref[:, hh * hd:(hh + 1) * hd] = acc_ref[hh].astype(o_ref.dtype)


def _sb_flash(q, k, v, batch, scale, t=256):
    heads, m, hd = q.shape
    seq = m // batch
    t = _tile(seq, t)
    nq = seq // t
    return pl.pallas_call(
        functools.partial(_sb_flash_kernel, scale=scale, seq=seq),
        grid=(batch, nq),
        in_specs=[pl.BlockSpec((heads, t, hd), lambda b, qi: (0, b * nq + qi, 0)),
                  pl.BlockSpec(memory_space=pl.ANY), pl.BlockSpec(memory_space=pl.ANY)],
        out_specs=pl.BlockSpec((t, heads * hd), lambda b, qi: (b * nq + qi, 0)),
        out_shape=jax.ShapeDtypeStruct((m, heads * hd), BF16),
        scratch_shapes=[pltpu.VMEM((2, heads, t, hd), BF16), pltpu.VMEM((2, heads, t, hd), BF16),
                        pltpu.SemaphoreType.DMA((2, 2)),
                        pltpu.VMEM((heads, t, LANES), F32), pltpu.VMEM((heads, t, hd), F32)],
        compiler_params=_cparams("parallel", "parallel"),
        name="sb_flash",
    )(q, k, v)


def _sb_sample_kernel(q_ref, kn_ref, vn_ref, kc_hbm, vc_hbm, o_ref, kbuf, vbuf, sem, r_ref, acc_ref, *,
                      scale, layer, blk):
    b = pl.program_id(0)
    nh, t, hd = acc_ref.shape
    nblk = kc_hbm.shape[2] // blk

    def copies(kb, slot):
        start = pl.multiple_of((nblk - 1 - kb) * blk, blk)
        cps = []
        for hh in range(nh):
            cps.append(pltpu.make_async_copy(kc_hbm.at[layer, b, pl.ds(start, blk), hh], kbuf.at[slot, hh],
                                             sem.at[0, slot]))
            cps.append(pltpu.make_async_copy(vc_hbm.at[layer, b, pl.ds(start, blk), hh], vbuf.at[slot, hh],
                                             sem.at[1, slot]))
        return cps

    for n, cp in enumerate(copies(0, 0)):
        cp.start(priority=n % 2)
    row = lax.broadcasted_iota(jnp.int32, (t, t), 0)
    col = lax.broadcasted_iota(jnp.int32, (t, t), 1)
    own_mask, own_tri = col < row, _tri(t)

    def own(hh, carry):
        o, r = _sb_block(q_ref[hh], kn_ref[hh], vn_ref[hh], own_mask, own_tri, jnp.zeros((t, LANES), F32), scale)
        acc_ref[hh] = o
        r_ref[hh] = r
        return carry

    lax.fori_loop(0, nh, own, 0)
    tri = _tri(blk)

    def sweep(carry):
        kb, _ = carry
        slot = kb % 2
        for cp in copies(kb, slot):
            cp.wait()

        @pl.when(kb + 1 < nblk)
        def _():
            for n, cp in enumerate(copies(kb + 1, 1 - slot)):
                cp.start(priority=n % 2)

        def head(hh, r_min):
            o, r = _sb_block(q_ref[hh], kbuf[slot, hh].astype(BF16), vbuf[slot, hh].astype(BF16), None, tri,
                             r_ref[hh], scale)
            acc_ref[hh] += o
            r_ref[hh] = r
            return jnp.minimum(r_min, jnp.min(r))

        r_min = lax.fori_loop(0, nh, head, jnp.float32(jnp.inf))
        return kb + 1, (r_min > SB_EXIT_SUM).astype(jnp.int32)

    kb_end, _ = lax.while_loop(lambda c: (c[0] < nblk) & (c[1] == 0), sweep, (jnp.int32(0), jnp.int32(0)))

    @pl.when(kb_end < nblk)
    def _():
        for cp in copies(kb_end, kb_end % 2):
            cp.wait()

    for hh in range(nh):
        o_ref[:, hh * hd:(hh + 1) * hd] = acc_ref[hh].astype(o_ref.dtype)


def _sb_sample(q, k_new, v_new, cache_k, cache_v, layer, scale, blk=256):
    heads, m, hd = q.shape
    batch, past = cache_k.shape[1], cache_k.shape[2]
    t = m // batch
    blk = _tile(past, blk)
    new = lambda b: (0, b, 0)
    return pl.pallas_call(
        functools.partial(_sb_sample_kernel, scale=scale, layer=layer, blk=blk),
        grid=(batch,),
        in_specs=[pl.BlockSpec((heads, t, hd), new), pl.BlockSpec((heads, t, hd), new),
                  pl.BlockSpec((heads, t, hd), new),
                  pl.BlockSpec(memory_space=pl.ANY), pl.BlockSpec(memory_space=pl.ANY)],
        out_specs=pl.BlockSpec((t, heads * hd), lambda b: (b, 0)),
        out_shape=jax.ShapeDtypeStruct((m, heads * hd), BF16),
        scratch_shapes=[pltpu.VMEM((2, heads, blk, hd), F32), pltpu.VMEM((2, heads, blk, hd), F32),
                        pltpu.SemaphoreType.DMA((2, 2)),
                        pltpu.VMEM((heads, t, LANES), F32), pltpu.VMEM((heads, t, hd), F32)],
        compiler_params=_cparams("parallel"),
        name="sb_sample",
    )(q, k_new, v_new, cache_k, cache_v)


def _rope_tables(pos, d, reps):
    half = d // 2
    inv = ROPE_BASE ** (-jnp.arange(half, dtype=F32) * 2.0 / d)
    ang = pos.astype(F32)[:, None] * inv[None, :]
    return jnp.tile(jnp.cos(ang), (1, reps)), jnp.tile(jnp.sin(ang), (1, reps))


def _rot_half_cols(w, d):
    k, n = w.shape
    w = w.reshape(k, n // d, d)
    half = d // 2
    return jnp.concatenate([-w[..., half:], w[..., :half]], axis=-1).reshape(k, n)


def kernel(x_prompt, x_sample, state_ret, cache_mla_ckv, cache_mla_krope, state_s5_re, state_s5_im, cache_sb_k, cache_sb_v, state_ffn_conv, norm_gains, ret_w_in, ret_w_out, mla_w_down, mla_g_q, mla_g_kv, mla_w_uq, mla_w_uk, mla_w_uv, mla_w_o, s5_a_re, s5_a_im, s5_log_dt, s5_b_re, s5_b_im, s5_c_re, s5_c_im, s5_d, s5_w_glu, s5_b_glu, sb_w_qkv, sb_w_o, ffn_w_up, ffn_conv_w, ffn_conv_b, ffn_w_down):
    n_p, seq_p, d = x_prompt.shape
    n_s, seq_s, _ = x_sample.shape
    depth = norm_gains.shape[0]
    n_mixers = 4
    streams = (dict(batch=n_p, seq=seq_p, pos=jnp.arange(seq_p)),
               dict(batch=n_s, seq=seq_s, pos=None))
    xs = [x_prompt.reshape(n_p * seq_p, d), x_sample.reshape(n_s * seq_s, d)]
    hs = [_rmsnorm(x, norm_gains[0, 0]) for x in xs]
    outs = {name: ([], []) for name in ('ret', 'ckv', 'kr', 's5re', 's5im', 'sbk', 'sbv', 'conv')}

    past_len = cache_mla_ckv.shape[2]

    def positions(si):
        if si == 0:
            return jnp.arange(seq_p)
        return jnp.tile(past_len + jnp.arange(seq_s), n_s)

    for i in range(depth):
        j, kind = i // n_mixers, i % n_mixers
        g = norm_gains[i]
        g_next = norm_gains[i + 1, 0] if i + 1 < depth else g[0]
        emit_next = i + 1 < depth and (i + 1) % n_mixers != 2
        ffn_up = ffn_w_up[i].astype(BF16)
        ffn_down = ffn_w_down[i].astype(BF16)
        for si, st in enumerate(streams):
            x, h, batch = xs[si], hs[si], st['batch']
            if kind == 0:
                heads, dk, dv = state_ret.shape[2], state_ret.shape[3], state_ret.shape[4]
                qk_w, v_w = heads * dk, heads * dv
                w_in = ret_w_in[j].astype(BF16)
                cos, sin = _rope_tables(positions(si), dk, 1)
                qk = _linear_rope(h, w_in, 2 * qk_w, cos, sin, dk, qk_w, dk ** -0.5)
                v, = _linear(h, w_in, 2 * qk_w, v_w, (BF16,), name="ret_v")
                gate, = _linear(h, w_in, 2 * qk_w + v_w, v_w, (F32,), name="ret_g")
                s0 = state_ret[j] if si == 1 else None
                y, s1 = _retention(qk, v, gate, batch, heads, s0)
                outs['ret'][si].append(s1)
                x, h = _outproj(y, ret_w_out[j].astype(BF16), x, g[1], g[2], name="ret_outproj")
            elif kind == 1:
                ql, kvl = mla_g_q.shape[1], mla_g_kv.shape[1]
                heads, nope = mla_w_uk.shape[2], mla_w_uk.shape[3]
                rd, vd = cache_mla_krope.shape[3], mla_w_uv.shape[3]
                past = cache_mla_ckv.shape[2]
                scale = (nope + rd) ** -0.5
                w_down = mla_w_down[j]
                w_kr = w_down[:, ql + kvl:]
                pos = positions(si)
                cos_k, sin_k = _rope_tables(pos, rd, 2)
                cos_q, sin_q = _rope_tables(pos, rd, 2 * LANES // rd)
                cq, ckv, kr = _mla_down(h, w_down[:, :ql].astype(BF16), w_down[:, ql:ql + kvl].astype(BF16),
                                        w_kr.astype(BF16), _rot_half_cols(w_kr, rd).astype(BF16),
                                        mla_g_q[j], mla_g_kv[j], cos_k, sin_k)
                w_uq = mla_w_uq[j].reshape(ql, heads, nope + rd)
                w_qn = w_uq[:, :, :nope].reshape(ql, heads * nope)
                w_qr = w_uq[:, :, nope:].reshape(ql, heads * rd)
                q = _mla_q(cq, w_qn.astype(BF16), w_qr.astype(BF16), _rot_half_cols(w_qr, rd).astype(BF16),
                           cos_q, sin_q, heads, scale)
                outs['ckv'][si].append(ckv.reshape(batch, st['seq'], kvl))
                outs['kr'][si].append(kr.reshape(batch, st['seq'], rd))
                if si == 0:
                    k, v = _mla_kv(ckv, kr, mla_w_uk[j].reshape(kvl, heads * nope).astype(BF16),
                                   mla_w_uv[j].reshape(kvl, heads * vd).astype(BF16), heads)
                    o = _mla_flash(q, k, v, batch)
                else:
                    o = _mla_sample(q, ckv, kr, cache_mla_ckv[j], cache_mla_krope[j],
                                    jnp.transpose(mla_w_uk[j], (1, 0, 2)).astype(BF16),
                                    jnp.transpose(mla_w_uv[j], (1, 0, 2)).astype(BF16))
                x, h = _outproj(o, mla_w_o[j].astype(BF16), x, g[1], g[2], name="mla_outproj")
            elif kind == 2:
                params = _s5_params(s5_a_re[j], s5_a_im[j], s5_log_dt[j], s5_b_re[j], s5_b_im[j],
                                    s5_c_re[j], s5_c_im[j])
                h0 = (state_s5_re[j], state_s5_im[j]) if si == 1 else None
                y, s_re, s_im = _s5(x, g[0], *params, s5_d[j], batch, h0)
                outs['s5re'][si].append(s_re.reshape(batch, *s5_a_re.shape[1:]))
                outs['s5im'][si].append(s_im.reshape(batch, *s5_a_re.shape[1:]))
                x, h = _glu_outproj(y, s5_w_glu[j].astype(BF16), s5_b_glu[j], x, g[1], g[2])
            else:
                heads, hd = cache_sb_k.shape[3], cache_sb_k.shape[4]
                past = cache_sb_k.shape[2]
                w_qkv = sb_w_qkv[j].astype(BF16)
                scale = hd ** -0.5
                q, = _linear(h, w_qkv, 0, heads * hd, (BF16,), (True,), name="sb_q")
                k32, k16 = _linear(h, w_qkv, heads * hd, heads * hd, (F32, BF16), (False, True), name="sb_k")
                v32, v16 = _linear(h, w_qkv, 2 * heads * hd, heads * hd, (F32, BF16), (False, True), name="sb_v")
                outs['sbk'][si].append(k32.reshape(batch, st['seq'], heads, hd))
                outs['sbv'][si].append(v32.reshape(batch, st['seq'], heads, hd))
                if si == 0:
                    o = _sb_flash(q, k16, v16, batch, scale)
                else:
                    o = _sb_sample(q, k16, v16, cache_sb_k, cache_sb_v, j, scale)
                x, h = _outproj(o, sb_w_o[j].astype(BF16), x, g[1], g[2], name="sb_outproj")
            x, h, conv = _conv_ffn(h, x, ffn_up, ffn_conv_w[i], ffn_conv_b[i], ffn_down, g[3], g_next, st['seq'],
                                   state=state_ffn_conv[i] if si == 1 else None, emit_h=emit_next)
            outs['conv'][si].append(conv)
            xs[si], hs[si] = x, h

    stack = lambda name: (jnp.stack(outs[name][0]), jnp.stack(outs[name][1]))
    return (xs[0].reshape(x_prompt.shape), xs[1].reshape(x_sample.shape),
            *stack('ret'), *stack('ckv'), *stack('kr'), *stack('s5re'), *stack('s5im'),
            *stack('sbk'), *stack('sbv'), *stack('conv'))
```
